```python
import math
import jax
import jax.numpy as jnp
from jax import lax
import numpy as np

D_MODEL = 1024
BATCH = 2
SEQ = 8192
DEPTH = 4
DEC_BATCH = 128
DEC_SEQ = 1
PAST_LEN = 2048
PAGE_SIZE = 128

HEAD_DIM = 64
N_HEADS = D_MODEL // HEAD_DIM
DIL_GROUPS = ((128, 1), (512, 4), (2048, 16))
N_GROUPS = len(DIL_GROUPS)
H_DIFF = D_MODEL // (2 * HEAD_DIM)
N_BUCKETS = 32
MAX_DISTANCE = 2048
D_FF = 2752
N_EXPERTS = 8
TOP_K = 2
D_PLE = 256
Q_BLOCK = 128
LN_EPS = 1e-5
RMS_EPS = 1e-5
N_MIXERS = 3
ALPHA = (2 * DEPTH) ** 0.25
BETA = (8 * DEPTH) ** -0.25
N_A = (DEPTH + 2) // 3
N_B = (DEPTH + 1) // 3
N_C = DEPTH // 3
N_DENSE = (DEPTH + 1) // 2
N_MOE = DEPTH // 2
ATTN_SCALE = HEAD_DIM ** -0.5

kernel_name = 'hybrid_dilated_diff_fox_decoder_step'


def layer_norm(x, g, b):
    xf = x.astype(jnp.float32)
    mu = jnp.mean(xf, axis=-1, keepdims=True)
    var = jnp.mean(jnp.square(xf - mu), axis=-1, keepdims=True)
    y = (xf - mu) * lax.rsqrt(var + LN_EPS) * g.astype(jnp.float32) + b.astype(jnp.float32)
    return y.astype(x.dtype)


def rel_bucket(dist):
    dist = jnp.maximum(dist, 0)
    max_exact = N_BUCKETS // 2
    df = jnp.maximum(dist, 1).astype(jnp.float32)
    large = max_exact + (jnp.log(df / max_exact) / math.log(MAX_DISTANCE / max_exact)
                         * (N_BUCKETS - max_exact)).astype(jnp.int32)
    large = jnp.clip(large, 0, N_BUCKETS - 1)
    return jnp.where(dist < max_exact, dist, large)


def dilated_window_prompt(q, k, v, rel_bias, window, dil):
    bsz, seq, nh, hd = q.shape
    n = window // dil
    s_pad = -(-seq // window) * window
    pad = s_pad - seq
    nb = s_pad // window

    def to_blocks(t):
        t = jnp.pad(t, ((0, 0), (pad, 0), (0, 0), (0, 0)))
        t = t.reshape(bsz, s_pad // dil, dil, nh, hd).transpose(0, 2, 1, 3, 4)
        return t.reshape(bsz, dil, nb, n, nh, hd)

    def with_prev(t):
        prev = jnp.pad(t, ((0, 0), (0, 0), (1, 0), (0, 0), (0, 0), (0, 0)))[:, :, :-1]
        return jnp.concatenate([prev, t], axis=3)

    qb = to_blocks(q)
    kb = with_prev(to_blocks(k))
    vb = with_prev(to_blocks(v))
    steps = n + jnp.arange(n)[:, None] - jnp.arange(2 * n)[None, :]
    bias = rel_bias[rel_bucket(steps * dil)].transpose(2, 0, 1).astype(jnp.float32)
    key_sub = jnp.arange(nb)[:, None] * n + jnp.arange(2 * n)[None, :] - n
    key_pos = key_sub[None] * dil + jnp.arange(dil)[:, None, None] - pad
    band = (steps >= 0) & (steps <= n)
    mask = band[None, None] & ((key_pos >= 0)[:, :, None, :] | (steps == 0)[None, None])
    logits = jnp.einsum('brnqhe,brnkhe->brnhqk', qb, kb).astype(jnp.float32) * ATTN_SCALE + bias
    logits = jnp.where(mask[None, :, :, None], logits, -jnp.inf)
    lse = jax.nn.logsumexp(logits, axis=-1)
    probs = jnp.exp(logits - lse[..., None])
    out = jnp.einsum('brnhqk,brnkhe->brnqhe', probs.astype(v.dtype), vb)
    out = out.reshape(bsz, dil, s_pad // dil, nh, hd).transpose(0, 2, 1, 3, 4).reshape(bsz, s_pad, nh, hd)
    lse = lse.transpose(0, 1, 2, 4, 3).reshape(bsz, dil, s_pad // dil, nh).transpose(0, 2, 1, 3).reshape(bsz, s_pad, nh)
    return out[:, pad:], lse[:, pad:]


def dilated_window_sample(q, k_ext, v_ext, rel_bias, window, dil, n_buf):
    t_new = q.shape[1]
    n = window // dil
    steps = jnp.arange(n + 1)
    idx = n_buf + jnp.arange(t_new)[:, None] - dil * steps[None, :]
    valid = idx >= 0
    idx = jnp.maximum(idx, 0)
    kg = k_ext[:, idx]
    vg = v_ext[:, idx]
    bias = rel_bias[rel_bucket(dil * steps)].T.astype(jnp.float32)
    logits = jnp.einsum('bqhe,bqkhe->bhqk', q, kg).astype(jnp.float32) * ATTN_SCALE + bias[:, None, :]
    logits = jnp.where(valid[None, None], logits, -jnp.inf)
    lse = jax.nn.logsumexp(logits, axis=-1)
    probs = jnp.exp(logits - lse[..., None])
    out = jnp.einsum('bhqk,bqkhe->bqhe', probs.astype(v_ext.dtype), vg)
    return out, lse.transpose(0, 2, 1)


def merge_dilation_groups(outs, lses):
    w = jax.nn.softmax(jnp.stack(lses, axis=0), axis=0)
    return jnp.einsum('gbsh,gbshe->bshe', w.astype(outs[0].dtype), jnp.stack(outs, axis=0))


def mixer_a_prompt(x, w_qkv, w_o, rel_bias):
    bsz, seq, _ = x.shape
    qkv = jnp.einsum('bsd,df->bsf', x, w_qkv).reshape(bsz, seq, N_GROUPS, 3, N_HEADS, HEAD_DIM)
    outs, lses, bufs = [], [], []
    for g, (window, dil) in enumerate(DIL_GROUPS):
        o, l = dilated_window_prompt(qkv[:, :, g, 0], qkv[:, :, g, 1], qkv[:, :, g, 2], rel_bias, window, dil)
        outs.append(o)
        lses.append(l)
        bufs.append(qkv[:, seq - min(window, seq):, g, 1:3])
    y = merge_dilation_groups(outs, lses).reshape(bsz, seq, D_MODEL)
    return jnp.einsum('bsf,fd->bsd', y, w_o), tuple(bufs)


def mixer_a_sample(x, bufs, w_qkv, w_o, rel_bias):
    bsz, t_new, _ = x.shape
    qkv = jnp.einsum('bsd,df->bsf', x, w_qkv).reshape(bsz, t_new, N_GROUPS, 3, N_HEADS, HEAD_DIM)
    outs, lses, new_bufs = [], [], []
    for g, (window, dil) in enumerate(DIL_GROUPS):
        buf = bufs[g]
        n_buf = buf.shape[1]
        ext = jnp.concatenate([buf, qkv[:, :, g, 1:3].astype(buf.dtype)], axis=1)
        o, l = dilated_window_sample(qkv[:, :, g, 0], ext[:, :, 0], ext[:, :, 1], rel_bias, window, dil, n_buf)
        outs.append(o)
        lses.append(l)
        n_keep = min(window, ext.shape[1])
        new_bufs.append(ext[:, ext.shape[1] - n_keep:])
    y = merge_dilation_groups(outs, lses).reshape(bsz, t_new, D_MODEL)
    return jnp.einsum('bsf,fd->bsd', y, w_o), tuple(new_bufs)


def gather_pages(cache, page_table):
    g = cache[page_table]
    return g.reshape((g.shape[0], g.shape[1] * g.shape[2]) + g.shape[3:])


def diff_lambda(lam_params, layer_idx):
    lam_init = 0.8 - 0.6 * math.exp(-0.3 * layer_idx)
    lp = lam_params.astype(jnp.float32)
    lam = jnp.exp(jnp.sum(lp[0] * lp[1])) - jnp.exp(jnp.sum(lp[2] * lp[3])) + lam_init
    return lam, lam_init


def diff_core(q, k, v, q_pos, k_pos, rel_bias, lam):
    tq, tk = q.shape[1], k.shape[1]
    dist = q_pos[:, None] - k_pos[None, :]
    bias = rel_bias[rel_bucket(dist)].reshape(tq, tk, H_DIFF, 2).transpose(2, 3, 0, 1).astype(jnp.float32)
    logits = jnp.einsum('bqhme,bkhme->bhmqk', q, k).astype(jnp.float32) * ATTN_SCALE + bias
    logits = jnp.where(dist >= 0, logits, -jnp.inf)
    probs = jax.nn.softmax(logits, axis=-1)
    attn = probs[:, :, 0] - lam * probs[:, :, 1]
    return jnp.einsum('bhqk,bkhe->bqhe', attn.astype(v.dtype), v)


def diff_out(o, subln_g, lam_init, w_o):
    bsz, t = o.shape[:2]
    of = o.astype(jnp.float32)
    of = of * lax.rsqrt(jnp.mean(jnp.square(of), axis=-1, keepdims=True) + RMS_EPS)
    of = of * subln_g.astype(jnp.float32) * (1.0 - lam_init)
    return jnp.einsum('bsf,fd->bsd', of.reshape(bsz, t, D_MODEL).astype(o.dtype), w_o)


def mixer_b_prompt(x, w_qkv, lam_params, subln_g, w_o, rel_bias, layer_idx):
    bsz, seq, _ = x.shape
    qkv = jnp.einsum('bsd,df->bsf', x, w_qkv).reshape(bsz, seq, 3, H_DIFF, 2 * HEAD_DIM)
    q = qkv[:, :, 0].reshape(bsz, seq, H_DIFF, 2, HEAD_DIM)
    k = qkv[:, :, 1].reshape(bsz, seq, H_DIFF, 2, HEAD_DIM)
    v = qkv[:, :, 2]
    lam, lam_init = diff_lambda(lam_params, layer_idx)
    k_pos = jnp.arange(seq)

    def block(i):
        start = i * Q_BLOCK
        qb = lax.dynamic_slice_in_dim(q, start, Q_BLOCK, axis=1)
        return diff_core(qb, k, v, start + jnp.arange(Q_BLOCK), k_pos, rel_bias, lam)

    o = lax.map(block, jnp.arange(seq // Q_BLOCK))
    o = o.transpose(1, 0, 2, 3, 4).reshape(bsz, seq, H_DIFF, 2 * HEAD_DIM)
    return diff_out(o, subln_g, lam_init, w_o), (qkv[:, :, 1:3],)


def mixer_b_sample(x, cache_kv, page_table, w_qkv, lam_params, subln_g, w_o, rel_bias, layer_idx):
    bsz, t_new, _ = x.shape
    qkv = jnp.einsum('bsd,df->bsf', x, w_qkv).reshape(bsz, t_new, 3, H_DIFF, 2 * HEAD_DIM)
    q = qkv[:, :, 0].reshape(bsz, t_new, H_DIFF, 2, HEAD_DIM)
    past = gather_pages(cache_kv, page_table)
    n_past = past.shape[1]
    ext = jnp.concatenate([past, qkv[:, :, 1:3].astype(past.dtype)], axis=1)
    k = ext[:, :, 0].reshape(bsz, n_past + t_new, H_DIFF, 2, HEAD_DIM)
    v = ext[:, :, 1]
    lam, lam_init = diff_lambda(lam_params, layer_idx)
    o = diff_core(q, k, v, n_past + jnp.arange(t_new), jnp.arange(n_past + t_new), rel_bias, lam)
    return diff_out(o, subln_g, lam_init, w_o), (qkv[:, :, 1:3],)


def forget_log(x, w_f, b_f):
    return jax.nn.log_sigmoid((jnp.einsum('bsd,dh->bsh', x, w_f) + b_f).astype(jnp.float32))


def fox_core(q, k, v, q_pos, k_pos, cum_q, cum_k):
    decay = cum_q.transpose(0, 2, 1)[:, :, :, None] - cum_k.transpose(0, 2, 1)[:, :, None, :]
    logits = jnp.einsum('bqhe,bkhe->bhqk', q, k).astype(jnp.float32) * ATTN_SCALE + decay
    logits = jnp.where(q_pos[:, None] >= k_pos[None, :], logits, -jnp.inf)
    probs = jax.nn.softmax(logits, axis=-1)
    return jnp.einsum('bhqk,bkhe->bqhe', probs.astype(v.dtype), v)


def mixer_c_prompt(x, w_qkv, w_f, b_f, w_o):
    bsz, seq, _ = x.shape
    qkv = jnp.einsum('bsd,df->bsf', x, w_qkv).reshape(bsz, seq, 3, N_HEADS, HEAD_DIM)
    q, k, v = qkv[:, :, 0], qkv[:, :, 1], qkv[:, :, 2]
    logf = forget_log(x, w_f, b_f)
    cum = jnp.cumsum(logf, axis=1)
    k_pos = jnp.arange(seq)

    def block(i):
        start = i * Q_BLOCK
        qb = lax.dynamic_slice_in_dim(q, start, Q_BLOCK, axis=1)
        cq = lax.dynamic_slice_in_dim(cum, start, Q_BLOCK, axis=1)
        return fox_core(qb, k, v, start + jnp.arange(Q_BLOCK), k_pos, cq, cum)

    o = lax.map(block, jnp.arange(seq // Q_BLOCK))
    o = o.transpose(1, 0, 2, 3, 4).reshape(bsz, seq, D_MODEL)
    return jnp.einsum('bsf,fd->bsd', o, w_o), (qkv[:, :, 1:3], logf)


def mixer_c_sample(x, cache_kv, cache_logf, page_table, w_qkv, w_f, b_f, w_o):
    bsz, t_new, _ = x.shape
    qkv = jnp.einsum('bsd,df->bsf', x, w_qkv).reshape(bsz, t_new, 3, N_HEADS, HEAD_DIM)
    logf = forget_log(x, w_f, b_f)
    past_kv = gather_pages(cache_kv, page_table)
    past_logf = gather_pages(cache_logf, page_table)
    n_past = past_kv.shape[1]
    ext = jnp.concatenate([past_kv, qkv[:, :, 1:3].astype(past_kv.dtype)], axis=1)
    cum = jnp.cumsum(jnp.concatenate([past_logf.astype(jnp.float32), logf], axis=1), axis=1)
    o = fox_core(qkv[:, :, 0], ext[:, :, 0], ext[:, :, 1], n_past + jnp.arange(t_new),
                 jnp.arange(n_past + t_new), cum[:, n_past:], cum)
    o = o.reshape(bsz, t_new, D_MODEL)
    return jnp.einsum('bsf,fd->bsd', o, w_o), (qkv[:, :, 1:3], logf)


def swiglu(x, w_gu, w_down):
    g, u = jnp.split(jnp.einsum('bsd,df->bsf', x, w_gu), 2, axis=-1)
    return jnp.einsum('bsf,fd->bsd', jax.nn.silu(g) * u, w_down)


def moe_ffn(x, w_router, w_gu, w_down):
    logits = jnp.einsum('bsd,de->bse', x, w_router).astype(jnp.float32)
    top_v, top_i = lax.top_k(logits, TOP_K)
    gates = jax.nn.softmax(top_v, axis=-1)
    combine = jnp.einsum('bsk,bske->bse', gates, jax.nn.one_hot(top_i, N_EXPERTS, dtype=jnp.float32))
    y = jnp.zeros(x.shape, jnp.float32)
    for e in range(N_EXPERTS):
        y = y + combine[..., e:e + 1] * swiglu(x, w_gu[e], w_down[e]).astype(jnp.float32)
    return y.astype(x.dtype)


def post_norm(x, sub, g, b):
    return layer_norm(ALPHA * x + sub.astype(x.dtype), g, b)


def per_layer_embed(x, p, w_in, w_gate, b_gate):
    gate = jax.nn.sigmoid(jnp.einsum('bsd,df->bsf', x, w_gate) + b_gate)
    return (x + gate * jnp.einsum('bsp,pd->bsd', p, w_in)).astype(x.dtype)


def setup_inputs(seed: int = 0) -> dict:
    key = jax.random.key(seed)
    keys = iter(jax.random.split(key, 64))
    f32 = jnp.float32

    def normal(shape, scale):
        return scale * jax.random.normal(next(keys), shape, f32)

    d = D_MODEL
    s_in = d ** -0.5
    n_pages = PAST_LEN // PAGE_SIZE
    n_pool = (DEC_BATCH * n_pages * 5) // 4
    w128, w512, w2048 = (w for w, _ in DIL_GROUPS)

    def win_buf(w):
        return normal((DEC_BATCH, min(w, PAST_LEN), 2, N_HEADS, HEAD_DIM), 1.0)

    inp = {}
    inp['x_prompt'] = normal((BATCH, SEQ, d), 1.0)
    inp['x_sample'] = normal((DEC_BATCH, DEC_SEQ, d), 1.0)
    inp['cache_l0_kv_w128'] = win_buf(w128)
    inp['cache_l0_kv_w512'] = win_buf(w512)
    inp['cache_l0_kv_w2048'] = win_buf(w2048)
    inp['cache_l1_kv'] = normal((n_pool, PAGE_SIZE, 2, H_DIFF, 2 * HEAD_DIM), 1.0)
    inp['cache_l2_kv'] = normal((n_pool, PAGE_SIZE, 2, N_HEADS, HEAD_DIM), 1.0)
    inp['cache_l2_logf'] = -0.2 * jax.random.uniform(next(keys), (n_pool, PAGE_SIZE, N_HEADS), f32)
    inp['cache_l3_kv_w128'] = win_buf(w128)
    inp['cache_l3_kv_w512'] = win_buf(w512)
    inp['cache_l3_kv_w2048'] = win_buf(w2048)
    perm = jax.random.permutation(next(keys), n_pool)
    inp['page_table'] = perm[: DEC_BATCH * n_pages].reshape(DEC_BATCH, n_pages).astype(jnp.int32)
    inp['p_prompt'] = normal((DEPTH, BATCH, SEQ, D_PLE), 1.0)
    inp['p_sample'] = normal((DEPTH, DEC_BATCH, DEC_SEQ, D_PLE), 1.0)
    inp['rel_bias'] = normal((N_BUCKETS, N_HEADS), 0.5)
    inp['ln_g'] = 1.0 + normal((DEPTH, 2, d), 0.02)
    inp['ln_b'] = normal((DEPTH, 2, d), 0.02)
    inp['ple_w_in'] = normal((DEPTH, D_PLE, d), D_PLE ** -0.5)
    inp['ple_w_gate'] = normal((DEPTH, d, d), s_in)
    inp['ple_b_gate'] = normal((DEPTH, d), 0.02)
    inp['a_w_qkv'] = normal((N_A, d, N_GROUPS * 3 * N_HEADS * HEAD_DIM), s_in)
    inp['a_w_o'] = normal((N_A, d, d), s_in * BETA)
    inp['b_w_qkv'] = normal((N_B, d, 3 * d), s_in)
    inp['b_lambda'] = normal((N_B, 4, HEAD_DIM), 0.1)
    inp['b_subln_g'] = 1.0 + normal((N_B, 2 * HEAD_DIM), 0.02)
    inp['b_w_o'] = normal((N_B, d, d), s_in * BETA)
    inp['c_w_qkv'] = normal((N_C, d, 3 * d), s_in)
    inp['c_w_f'] = normal((N_C, d, N_HEADS), s_in)
    inp['c_b_f'] = 2.0 + normal((N_C, N_HEADS), 0.5)
    inp['c_w_o'] = normal((N_C, d, d), s_in * BETA)
    inp['ffn_w_gu'] = normal((N_DENSE, d, 2 * D_FF), s_in)
    inp['ffn_w_down'] = normal((N_DENSE, D_FF, d), D_FF ** -0.5 * BETA)
    inp['moe_router'] = normal((N_MOE, d, N_EXPERTS), s_in)
    inp['moe_w_gu'] = normal((N_MOE, N_EXPERTS, d, 2 * D_FF), s_in)
    inp['moe_w_down'] = normal((N_MOE, N_EXPERTS, D_FF, d), D_FF ** -0.5 * BETA)
    return inp


def reference(x_prompt, x_sample,
              cache_l0_kv_w128, cache_l0_kv_w512, cache_l0_kv_w2048,
              cache_l1_kv, cache_l2_kv, cache_l2_logf,
              cache_l3_kv_w128, cache_l3_kv_w512, cache_l3_kv_w2048,
              page_table, p_prompt, p_sample,
              rel_bias, ln_g, ln_b, ple_w_in, ple_w_gate, ple_b_gate,
              a_w_qkv, a_w_o, b_w_qkv, b_lambda, b_subln_g, b_w_o,
              c_w_qkv, c_w_f, c_b_f, c_w_o,
              ffn_w_gu, ffn_w_down, moe_router, moe_w_gu, moe_w_down):
    caches = ((cache_l0_kv_w128, cache_l0_kv_w512, cache_l0_kv_w2048),
              (cache_l1_kv,),
              (cache_l2_kv, cache_l2_logf),
              (cache_l3_kv_w128, cache_l3_kv_w512, cache_l3_kv_w2048))
    hp, hs = x_prompt, x_sample
    states = []
    for i in range(DEPTH):
        kind = i % N_MIXERS
        j = i // N_MIXERS
        if kind == 0:
            mp, st_p = mixer_a_prompt(hp, a_w_qkv[j], a_w_o[j], rel_bias)
            ms, st_s = mixer_a_sample(hs, caches[i], a_w_qkv[j], a_w_o[j], rel_bias)
        elif kind == 1:
            mp, st_p = mixer_b_prompt(hp, b_w_qkv[j], b_lambda[j], b_subln_g[j], b_w_o[j], rel_bias, i)
            ms, st_s = mixer_b_sample(hs, caches[i][0], page_table, b_w_qkv[j], b_lambda[j],
                                      b_subln_g[j], b_w_o[j], rel_bias, i)
        else:
            mp, st_p = mixer_c_prompt(hp, c_w_qkv[j], c_w_f[j], c_b_f[j], c_w_o[j])
            ms, st_s = mixer_c_sample(hs, caches[i][0], caches[i][1], page_table,
                                      c_w_qkv[j], c_w_f[j], c_b_f[j], c_w_o[j])
        states.append((st_p, st_s))
        hp = post_norm(hp, mp, ln_g[i, 0], ln_b[i, 0])
        hs = post_norm(hs, ms, ln_g[i, 0], ln_b[i, 0])
        f = i // 2
        if i % 2 == 0:
            fp = swiglu(hp, ffn_w_gu[f], ffn_w_down[f])
            fs = swiglu(hs, ffn_w_gu[f], ffn_w_down[f])
        else:
            fp = moe_ffn(hp, moe_router[f], moe_w_gu[f], moe_w_down[f])
            fs = moe_ffn(hs, moe_router[f], moe_w_gu[f], moe_w_down[f])
        hp = post_norm(hp, fp, ln_g[i, 1], ln_b[i, 1])
        hs = post_norm(hs, fs, ln_g[i, 1], ln_b[i, 1])
        hp = per_layer_embed(hp, p_prompt[i], ple_w_in[i], ple_w_gate[i], ple_b_gate[i])
        hs = per_layer_embed(hs, p_sample[i], ple_w_in[i], ple_w_gate[i], ple_b_gate[i])
    (l0_w128_p, l0_w512_p, l0_w2048_p), (l0_w128_s, l0_w512_s, l0_w2048_s) = states[0]
    (l1_kv_p,), (l1_kv_s,) = states[1]
    (l2_kv_p, l2_logf_p), (l2_kv_s, l2_logf_s) = states[2]
    (l3_w128_p, l3_w512_p, l3_w2048_p), (l3_w128_s, l3_w512_s, l3_w2048_s) = states[3]
    return (hp, hs,
            l0_w128_p, l0_w128_s, l0_w512_p, l0_w512_s, l0_w2048_p, l0_w2048_s,
            l1_kv_p, l1_kv_s,
            l2_kv_p, l2_kv_s, l2_logf_p, l2_logf_s,
            l3_w128_p, l3_w128_s, l3_w512_p, l3_w512_s, l3_w2048_p, l3_w2048_s)
```

```python
import functools
import math

import numpy as np
import jax
import jax.numpy as jnp
from jax import lax
from jax.experimental import pallas as pl
from jax.experimental.pallas import tpu as pltpu

F32 = jnp.float32
BF16 = jnp.bfloat16

D_MODEL = 1024
HEAD_DIM = 64
N_HEADS = 16
H_DIFF = 8
DIL_GROUPS = ((128, 1), (512, 4), (2048, 16))
N_KEYS = 128
N_BUCKETS = 32
MAX_DISTANCE = 2048
D_FF = 2752
D_FF_PAD = 2816
N_EXPERTS = 8
D_PLE = 256
DEPTH = 4
PAGE_SIZE = 128
LN_EPS = 1e-5
RMS_EPS = 1e-5
ALPHA = (2 * DEPTH) ** 0.25
ATTN_SCALE = HEAD_DIM ** -0.5
NEG = -1e30
LANES = 128
FAR_DIST = 1520
VMEM_LIMIT = 56 * 1024 * 1024


def _params(*sem):
    return pltpu.CompilerParams(dimension_semantics=sem, vmem_limit_bytes=VMEM_LIMIT)


def _dot(a, b):
    return jnp.dot(a, b, preferred_element_type=F32)


def _dot_nt(a, b):
    return lax.dot_general(a, b, (((1,), (1,)), ((), ())), preferred_element_type=F32)


def _layer_norm(y, g, b):
    mu = jnp.mean(y, axis=-1, keepdims=True)
    yc = y - mu
    var = jnp.mean(yc * yc, axis=-1, keepdims=True)
    return yc * lax.rsqrt(var + LN_EPS) * g + b


def _sigmoid(x):
    return 1.0 / (1.0 + jnp.exp(-x))


def _linear_body(x_ref, w_ref, *o_refs, scale):
    acc = _dot(x_ref[...].astype(BF16), w_ref[...])
    if scale is not None:
        acc = acc * scale
    for o_ref in o_refs:
        o_ref[...] = acc.astype(o_ref.dtype)


def linear(x, w, out_dtypes, scale=None, tm=512, tn=1024):
    m, k = x.shape
    n = w.shape[1]
    tm, tn = min(tm, m), min(tn, n)
    assert m % tm == 0 and n % tn == 0
    outs = pl.pallas_call(
        functools.partial(_linear_body, scale=scale),
        grid=(m // tm, n // tn),
        in_specs=[pl.BlockSpec((tm, k), lambda i, j: (i, 0)),
                  pl.BlockSpec((k, tn), lambda i, j: (0, j))],
        out_specs=[pl.BlockSpec((tm, tn), lambda i, j: (i, j)) for _ in out_dtypes],
        out_shape=[jax.ShapeDtypeStruct((m, n), d) for d in out_dtypes],
        compiler_params=_params("parallel", "parallel"),
        name="linear",
    )(x, w)
    return outs


def _forget_body(x_ref, w_ref, b_ref, o_ref):
    z = _dot(x_ref[...].astype(BF16), w_ref[...]) + b_ref[...]
    o_ref[...] = -(jnp.maximum(-z, 0.0) + jnp.log1p(jnp.exp(-jnp.abs(z))))


def forget_log(x, w_pad, b_pad, tm=512):
    m, k = x.shape
    tm = min(tm, m)
    assert m % tm == 0
    return pl.pallas_call(
        _forget_body,
        grid=(m // tm,),
        in_specs=[pl.BlockSpec((tm, k), lambda i: (i, 0)),
                  pl.BlockSpec((k, LANES), lambda i: (0, 0)),
                  pl.BlockSpec((1, LANES), lambda i: (0, 0))],
        out_specs=pl.BlockSpec((tm, LANES), lambda i: (i, 0)),
        out_shape=jax.ShapeDtypeStruct((m, LANES), F32),
        compiler_params=_params("parallel"),
        name="forget_log",
    )(x, w_pad, b_pad)


def _linear_ln_body(a_ref, w_ref, r_ref, g_ref, b_ref, o_ref):
    sub = _dot(a_ref[...].astype(BF16), w_ref[...])
    o_ref[...] = _layer_norm(ALPHA * r_ref[...] + sub, g_ref[...], b_ref[...])


def linear_ln(a, w, resid, g, b, tm=512):
    m, k = a.shape
    n = w.shape[1]
    tm = min(tm, m)
    assert m % tm == 0
    return pl.pallas_call(
        _linear_ln_body,
        grid=(m // tm,),
        in_specs=[pl.BlockSpec((tm, k), lambda i: (i, 0)),
                  pl.BlockSpec((k, n), lambda i: (0, 0)),
                  pl.BlockSpec((tm, n), lambda i: (i, 0)),
                  pl.BlockSpec((1, n), lambda i: (0, 0)),
                  pl.BlockSpec((1, n), lambda i: (0, 0))],
        out_specs=pl.BlockSpec((tm, n), lambda i: (i, 0)),
        out_shape=jax.ShapeDtypeStruct((m, n), F32),
        compiler_params=_params("parallel"),
        name="linear_ln",
    )(a, w, resid, g.reshape(1, n), b.reshape(1, n))


def _ple_body(x_ref, p_ref, wg_ref, bg_ref, wp_ref, o_ref):
    x = x_ref[...]
    gate = _sigmoid(_dot(x.astype(BF16), wg_ref[...]) + bg_ref[...])
    o_ref[...] = x + gate * _dot(p_ref[...].astype(BF16), wp_ref[...])


def per_layer_embed(x, p, w_gate, b_gate, w_in, tm=512):
    m, d = x.shape
    dp = p.shape[1]
    tm = min(tm, m)
    assert m % tm == 0
    return pl.pallas_call(
        _ple_body,
        grid=(m // tm,),
        in_specs=[pl.BlockSpec((tm, d), lambda i: (i, 0)),
                  pl.BlockSpec((tm, dp), lambda i: (i, 0)),
                  pl.BlockSpec((d, d), lambda i: (0, 0)),
                  pl.BlockSpec((1, d), lambda i: (0, 0)),
                  pl.BlockSpec((dp, d), lambda i: (0, 0))],
        out_specs=pl.BlockSpec((tm, d), lambda i: (i, 0)),
        out_shape=jax.ShapeDtypeStruct((m, d), F32),
        compiler_params=_params("parallel"),
        name="per_layer_embed",
    )(x, p, w_gate, b_gate.reshape(1, d), w_in)


def _router_body(x_ref, w_ref, o_ref):
    logits = _dot(x_ref[...].astype(BF16), w_ref[...])
    lane = lax.broadcasted_iota(jnp.int32, logits.shape, 1)
    logits = jnp.where(lane < N_EXPERTS, logits, NEG)
    v1 = jnp.max(logits, axis=-1, keepdims=True)
    i1 = jnp.min(jnp.where(logits == v1, lane, LANES), axis=-1, keepdims=True)
    rest = jnp.where(lane == i1, NEG, logits)
    v2 = jnp.max(rest, axis=-1, keepdims=True)
    i2 = jnp.min(jnp.where(rest == v2, lane, LANES), axis=-1, keepdims=True)
    e2 = jnp.exp(v2 - v1)
    g1 = 1.0 / (1.0 + e2)
    g2 = e2 / (1.0 + e2)
    o_ref[...] = jnp.where(lane == i1, g1, 0.0) + jnp.where(lane == i2, g2, 0.0)


def moe_router(x, w_pad, tm=512):
    m, k = x.shape
    tm = min(tm, m)
    assert m % tm == 0
    return pl.pallas_call(
        _router_body,
        grid=(m // tm,),
        in_specs=[pl.BlockSpec((tm, k), lambda i: (i, 0)),
                  pl.BlockSpec((k, LANES), lambda i: (0, 0))],
        out_specs=pl.BlockSpec((tm, LANES), lambda i: (i, 0)),
        out_shape=jax.ShapeDtypeStruct((m, LANES), F32),
        compiler_params=_params("parallel"),
        name="moe_router",
    )(x, w_pad)


def _ffn_body(x_ref, wg_ref, wu_ref, wd_ref, c_ref, g_ref, b_ref, o_ref, xb_ref, acc_ref, *, gated):
    e, f = pl.program_id(1), pl.program_id(2)

    @pl.when((e == 0) & (f == 0))
    def _():
        xb_ref[...] = x_ref[...].astype(BF16)
        acc_ref[...] = jnp.zeros_like(acc_ref)

    xb = xb_ref[...]
    gate = _dot(xb, wg_ref[0])
    up = _dot(xb, wu_ref[0])
    h = (gate * _sigmoid(gate) * up).astype(BF16)
    y = _dot(h, wd_ref[0])
    if gated:
        c = c_ref[...]
        lane = lax.broadcasted_iota(jnp.int32, c.shape, 1)
        y = y * jnp.sum(jnp.where(lane == e, c, 0.0), axis=-1, keepdims=True)
    acc_ref[...] += y

    @pl.when((e == pl.num_programs(1) - 1) & (f == pl.num_programs(2) - 1))
    def _():
        o_ref[...] = _layer_norm(ALPHA * x_ref[...] + acc_ref[...], g_ref[...], b_ref[...])


def ffn_ln(x, w_gate, w_up, w_down, combine, g, b, tm=1024, tf=256):
    m, d = x.shape
    n_e, _, f_pad = w_gate.shape
    tm = min(tm, m)
    assert m % tm == 0 and f_pad % tf == 0
    gated = combine is not None
    if not gated:
        combine = jnp.zeros((m, LANES), F32)
    return pl.pallas_call(
        functools.partial(_ffn_body, gated=gated),
        grid=(m // tm, n_e, f_pad // tf),
        in_specs=[pl.BlockSpec((tm, d), lambda i, e, f: (i, 0)),
                  pl.BlockSpec((1, d, tf), lambda i, e, f: (e, 0, f)),
                  pl.BlockSpec((1, d, tf), lambda i, e, f: (e, 0, f)),
                  pl.BlockSpec((1, tf, d), lambda i, e, f: (e, f, 0)),
                  pl.BlockSpec((tm, LANES), lambda i, e, f: (i, 0)),
                  pl.BlockSpec((1, d), lambda i, e, f: (0, 0)),
                  pl.BlockSpec((1, d), lambda i, e, f: (0, 0))],
        out_specs=pl.BlockSpec((tm, d), lambda i, e, f: (i, 0)),
        out_shape=jax.ShapeDtypeStruct((m, d), F32),
        scratch_shapes=[pltpu.VMEM((tm, d), BF16), pltpu.VMEM((tm, d), F32)],
        compiler_params=_params("parallel", "arbitrary", "arbitrary"),
        name="ffn_ln",
    )(x, w_gate, w_up, w_down, combine, g.reshape(1, d), b.reshape(1, d))


def _rel_bucket(dist):
    dist = jnp.maximum(dist, 0)
    max_exact = N_BUCKETS // 2
    df = jnp.maximum(dist, 1).astype(F32)
    large = max_exact + (jnp.log(df / max_exact) / math.log(MAX_DISTANCE / max_exact)
                         * (N_BUCKETS - max_exact)).astype(jnp.int32)
    large = jnp.clip(large, 0, N_BUCKETS - 1)
    return jnp.where(dist < max_exact, dist, large)


def _bias_by_distance(rel_bias, max_dist):
    return rel_bias[_rel_bucket(jnp.arange(max_dist + 1))].T.astype(F32)


def _half_masks(shape):
    lane = lax.broadcasted_iota(jnp.int32, shape, 1)
    return lane < HEAD_DIM, lane >= HEAD_DIM


def _half_ones(shape):
    lo, hi = _half_masks(shape)
    return jnp.where(lo, 1.0, 0.0).astype(BF16), jnp.where(hi, 1.0, 0.0).astype(BF16)


def _block_diag_rows(q_row, mask):
    return jnp.where(mask, jnp.broadcast_to(q_row, mask.shape), 0.0).astype(BF16)


def _dil_prompt_body(q_ref, kp_ref, kc_ref, vp_ref, vc_ref, bias_ref, o_ref, lse_ref):
    n = N_KEYS
    first = pl.program_id(2) == 0
    pen = jnp.where(first, NEG, 0.0).astype(F32)
    masks = _half_masks((n, LANES))
    ones = _half_ones((n, LANES))
    for pr in range(N_HEADS // 2):
        sl = slice(pr * LANES, (pr + 1) * LANES)
        q, kp, kc, vp, vc = q_ref[:, sl], kp_ref[:, sl], kc_ref[:, sl], vp_ref[:, sl], vc_ref[:, sl]
        o_pair = jnp.zeros((n, LANES), F32)
        lse_pair = jnp.zeros((n, LANES), F32)
        for half, (msk, one) in enumerate(zip(masks, ones)):
            h = 2 * pr + half
            qm = q * one
            s_prev = _dot_nt(qm, kp) + bias_ref[h, :, :n] + pen
            s_cur = _dot_nt(qm, kc) + bias_ref[h, :, n:]
            mx = jnp.maximum(jnp.max(s_prev, axis=-1, keepdims=True), jnp.max(s_cur, axis=-1, keepdims=True))
            p_prev = jnp.exp(s_prev - mx)
            p_cur = jnp.exp(s_cur - mx)
            den = jnp.sum(p_prev, axis=-1, keepdims=True) + jnp.sum(p_cur, axis=-1, keepdims=True)
            acc = _dot(p_prev.astype(BF16), vp * one) + _dot(p_cur.astype(BF16), vc * one)
            o_pair = o_pair + acc / den
            lse_pair = jnp.where(msk, mx + jnp.log(den), lse_pair)
        o_ref[:, sl] = o_pair
        lse_ref[:, sl] = lse_pair


def dilated_prompt_group(qkv, bias_tile, group, dil, bsz, seq):
    n = N_KEYS
    width = qkv.shape[1]
    n_col = width // D_MODEL
    nb = seq // (dil * n)
    view = qkv.reshape(bsz * seq // dil, dil * width)

    def col(which):
        return lambda b, r, i: (b * nb + i, r * n_col + group * 3 + which)

    def col_prev(which):
        return lambda b, r, i: (b * nb + jnp.maximum(i - 1, 0), r * n_col + group * 3 + which)

    blk = (n, D_MODEL)
    o, lse = pl.pallas_call(
        _dil_prompt_body,
        grid=(bsz, dil, nb),
        in_specs=[pl.BlockSpec(blk, col(0)),
                  pl.BlockSpec(blk, col_prev(1)), pl.BlockSpec(blk, col(1)),
                  pl.BlockSpec(blk, col_prev(2)), pl.BlockSpec(blk, col(2)),
                  pl.BlockSpec((N_HEADS, n, 2 * n), lambda b, r, i: (0, 0, 0))],
        out_specs=[pl.BlockSpec(blk, lambda b, r, i: (b * nb + i, r)),
                   pl.BlockSpec(blk, lambda b, r, i: (b * nb + i, r))],
        out_shape=[jax.ShapeDtypeStruct((bsz * seq // dil, dil * D_MODEL), F32)] * 2,
        compiler_params=_params("parallel", "parallel", "arbitrary"),
        name="dilated_prompt_group",
    )(view, view, view, view, view, bias_tile)
    return o.reshape(bsz * seq, D_MODEL), lse.reshape(bsz * seq, D_MODEL)


def _merge_o_ln_body(o1, o2, o3, l1, l2, l3, w_ref, r_ref, g_ref, b_ref, out_ref):
    a, b, c = l1[...], l2[...], l3[...]
    mx = jnp.maximum(jnp.maximum(a, b), c)
    ea, eb, ec = jnp.exp(a - mx), jnp.exp(b - mx), jnp.exp(c - mx)
    y = (ea * o1[...] + eb * o2[...] + ec * o3[...]) / (ea + eb + ec)
    sub = _dot(y.astype(BF16), w_ref[...])
    out_ref[...] = _layer_norm(ALPHA * r_ref[...] + sub, g_ref[...], b_ref[...])


def merge_o_ln(outs, lses, w_o, resid, g, b, tm=256):
    m, d = resid.shape
    tm = min(tm, m)
    assert m % tm == 0
    row = pl.BlockSpec((tm, d), lambda i: (i, 0))
    vec = pl.BlockSpec((1, d), lambda i: (0, 0))
    return pl.pallas_call(
        _merge_o_ln_body,
        grid=(m // tm,),
        in_specs=[row] * 6 + [pl.BlockSpec((d, d), lambda i: (0, 0)), row, vec, vec],
        out_specs=row,
        out_shape=jax.ShapeDtypeStruct((m, d), F32),
        compiler_params=_params("parallel"),
        name="merge_o_ln",
    )(*outs, *lses, w_o, resid, g.reshape(1, d), b.reshape(1, d))


def _dil_bias_tiles(bias_dist):
    n = N_KEYS
    step = n + np.arange(n)[:, None] - np.arange(2 * n)[None, :]
    band = (step >= 0) & (step <= n)
    tiles = []
    for _, dil in DIL_GROUPS:
        idx = np.clip(step, 0, n) * dil
        t = jnp.take(bias_dist, jnp.asarray(idx.reshape(-1), jnp.int32), axis=1).reshape(N_HEADS, n, 2 * n)
        tiles.append(jnp.where(jnp.asarray(band)[None], t, NEG))
    return tiles


def _block_diag_mask(rows, width):
    r = lax.broadcasted_iota(jnp.int32, (rows, rows * width), 0)
    c = lax.broadcasted_iota(jnp.int32, (rows, rows * width), 1)
    return (c >= r * width) & (c < (r + 1) * width)


def _dil_sample_body(qkv_ref, b1_ref, b2_ref, b3_ref, bias_ref, bias0_ref, o_ref):
    mask = _block_diag_mask(N_HEADS, HEAD_DIM)
    outs, lses = [], []
    for g, buf_ref in enumerate((b1_ref, b2_ref, b3_ref)):
        base = g * 3 * D_MODEL
        k_new = qkv_ref[0, :, base + D_MODEL:base + 2 * D_MODEL].astype(BF16).astype(F32)
        v_new = qkv_ref[0, :, base + 2 * D_MODEL:base + 3 * D_MODEL].astype(BF16).astype(F32)
        q_bd = _block_diag_rows(qkv_ref[0, :, base:base + D_MODEL] * ATTN_SCALE, mask)
        k_buf = buf_ref[0, :, :D_MODEL].astype(BF16)
        v_buf = buf_ref[0, :, D_MODEL:].astype(BF16)
        s_buf = _dot_nt(q_bd, k_buf) + bias_ref[g]
        s_new = jnp.sum(q_bd.astype(F32) * k_new, axis=-1, keepdims=True) + bias0_ref[g][:, :1]
        mx = jnp.maximum(jnp.max(s_buf, axis=-1, keepdims=True), s_new)
        p_buf = jnp.exp(s_buf - mx)
        p_new = jnp.exp(s_new - mx)
        den = jnp.sum(p_buf, axis=-1, keepdims=True) + p_new
        acc = _dot(p_buf.astype(BF16), v_buf) + p_new.astype(BF16).astype(F32) * v_new
        outs.append(acc / den)
        lses.append(mx + jnp.log(den))
    mx = jnp.maximum(jnp.maximum(lses[0], lses[1]), lses[2])
    ws = [jnp.exp(l - mx) for l in lses]
    y = (ws[0] * outs[0] + ws[1] * outs[1] + ws[2] * outs[2]) / (ws[0] + ws[1] + ws[2])
    o_ref[0] = jnp.sum(jnp.where(mask, y, 0.0), axis=0, keepdims=True)


def dilated_sample(qkv, bufs, bias_buf, bias_self):
    bsz = qkv.shape[0]
    n = N_KEYS
    views = []
    for buf, (window, dil) in zip(bufs, DIL_GROUPS):
        assert buf.shape[1] == window
        views.append(buf.reshape(bsz, window // dil, dil * 2 * D_MODEL))
    width = qkv.shape[1]
    out = pl.pallas_call(
        _dil_sample_body,
        grid=(bsz,),
        in_specs=[pl.BlockSpec((1, 1, width), lambda b: (b, 0, 0))]
                 + [pl.BlockSpec((1, n, 2 * D_MODEL), lambda b: (b, 0, 0)) for _ in views]
                 + [pl.BlockSpec((3, N_HEADS, n), lambda b: (0, 0, 0)),
                    pl.BlockSpec((3, N_HEADS, LANES), lambda b: (0, 0, 0))],
        out_specs=pl.BlockSpec((1, 1, D_MODEL), lambda b: (b, 0, 0)),
        out_shape=jax.ShapeDtypeStruct((bsz, 1, D_MODEL), F32),
        compiler_params=_params("parallel"),
        name="dilated_sample",
    )(qkv.reshape(bsz, 1, width), *views, bias_buf, bias_self)
    return out.reshape(bsz, D_MODEL)


def _dil_sample_bias(bias_dist):
    n = N_KEYS
    buf, self_ = [], []
    for _, dil in DIL_GROUPS:
        dist = dil * (n - np.arange(n))
        buf.append(jnp.take(bias_dist, jnp.asarray(dist, jnp.int32), axis=1))
        self_.append(jnp.broadcast_to(bias_dist[:, :1], (N_HEADS, LANES)))
    return jnp.stack(buf), jnp.stack(self_)


def _diff_lambda(lp, lam_init):
    a = jnp.sum(lp[0:1] * lp[1:2], axis=-1, keepdims=True)
    b = jnp.sum(lp[2:3] * lp[3:4], axis=-1, keepdims=True)
    return jnp.exp(a) - jnp.exp(b) + lam_init


def _diff_prompt_body(q_ref, k_ref, v_ref, bias_ref, lam_ref, g_ref, o_ref, m_ref, l_ref, acc_ref,
                      *, t, n_near, lam_init):
    qi = pl.program_id(2)
    q = q_ref[...]
    lo, hi = _half_ones(q.shape)
    qs = (q * lo, q * hi)
    m_ref[...] = jnp.full(m_ref.shape, NEG, F32)
    l_ref[...] = jnp.zeros(l_ref.shape, F32)
    acc_ref[...] = jnp.zeros(acc_ref.shape, F32)

    def tile(kj, near):
        start = pl.multiple_of(kj * t, t)
        k = k_ref[pl.ds(start, t), :]
        v = v_ref[pl.ds(start, t), :]
        for slot in range(2):
            s = _dot_nt(qs[slot], k)
            if near:
                s = s + bias_ref[slot, qi - kj]
            m_prev = m_ref[slot]
            m_new = jnp.maximum(m_prev, jnp.max(s, axis=-1, keepdims=True))
            alpha = jnp.exp(m_prev - m_new)
            p = jnp.exp(s - m_new[:, :1])
            l_ref[slot] = alpha * l_ref[slot] + jnp.sum(p, axis=-1, keepdims=True)
            acc_ref[slot] = alpha * acc_ref[slot] + _dot(p.astype(BF16), v)
            m_ref[slot] = m_new

    n_far = jnp.maximum(qi - (n_near - 1), 0)

    def far_body(kj, c):
        tile(kj, False)
        return c

    def near_body(kj, c):
        tile(kj, True)
        return c

    lax.fori_loop(0, n_far, far_body, 0)
    lax.fori_loop(n_far, qi + 1, near_body, 0)

    lam = _diff_lambda(lam_ref[...], lam_init)
    o = acc_ref[0] / l_ref[0] - lam * (acc_ref[1] / l_ref[1])
    o = o * lax.rsqrt(jnp.mean(o * o, axis=-1, keepdims=True) + RMS_EPS)
    o_ref[...] = (o * g_ref[...] * (1.0 - lam_init)).astype(o_ref.dtype)


def diff_prompt(q, kv, bias_tiles, lam_params, subln_g, lam_init, bsz, seq, t=256):
    nq = seq // t
    n_near = bias_tiles.shape[1]
    body = functools.partial(_diff_prompt_body, t=t, n_near=n_near, lam_init=lam_init)
    return pl.pallas_call(
        body,
        grid=(bsz, H_DIFF, nq),
        in_specs=[pl.BlockSpec((t, LANES), lambda b, h, i: (b * nq + i, h)),
                  pl.BlockSpec((seq, LANES), lambda b, h, i: (b, h)),
                  pl.BlockSpec((seq, LANES), lambda b, h, i: (b, H_DIFF + h)),
                  pl.BlockSpec((2, n_near, t, t), lambda b, h, i: (h, 0, 0, 0)),
                  pl.BlockSpec((4, HEAD_DIM), lambda b, h, i: (0, 0)),
                  pl.BlockSpec((1, LANES), lambda b, h, i: (0, 0))],
        out_specs=pl.BlockSpec((t, LANES), lambda b, h, i: (b * nq + i, h)),
        out_shape=jax.ShapeDtypeStruct((bsz * seq, D_MODEL), BF16),
        scratch_shapes=[pltpu.VMEM((2, t, LANES), F32), pltpu.VMEM((2, t, LANES), F32),
                        pltpu.VMEM((2, t, LANES), F32)],
        compiler_params=_params("parallel", "parallel", "arbitrary"),
        name="diff_prompt",
    )(q, kv, kv, bias_tiles, lam_params, subln_g.reshape(1, LANES))


def _diff_bias_tiles(bias_dist, t):
    n_near = -(-(FAR_DIST + t - 1) // t)
    d = np.arange(n_near)[:, None, None] * t + np.arange(t)[None, :, None] - np.arange(t)[None, None, :]
    tiles = jnp.take(bias_dist, jnp.asarray(np.maximum(d, 0).reshape(-1), jnp.int32), axis=1)
    tiles = tiles.reshape(N_HEADS, n_near, t, t) - bias_dist[:, -1][:, None, None, None]
    return jnp.where(jnp.asarray(d >= 0)[None], tiles, NEG)


PAGES_PER_STEP = 8


def _paged_sample_body(pt_ref, qkv_ref, *refs, kind, lam_init):
    page_refs = refs[:PAGES_PER_STEP]
    add_ref, aux_ref, g_ref, o_ref, m_ref, l_ref, acc_ref = refs[PAGES_PER_STEP:]
    c = pl.program_id(1)
    rows = 2 * H_DIFF if kind == "diff" else N_HEADS

    if kind == "diff":
        r = lax.broadcasted_iota(jnp.int32, (rows, D_MODEL), 0)
        col = lax.broadcasted_iota(jnp.int32, (rows, D_MODEL), 1)
        start = jnp.where(r < H_DIFF, r * LANES, (r - H_DIFF) * LANES + HEAD_DIM)
        qmask = (col >= start) & (col < start + HEAD_DIM)
    else:
        qmask = _block_diag_mask(N_HEADS, HEAD_DIM)
    q_bd = _block_diag_rows(qkv_ref[0, :, :D_MODEL] * ATTN_SCALE, qmask)

    @pl.when(c == 0)
    def _():
        m_ref[...] = jnp.full(m_ref.shape, NEG, F32)
        l_ref[...] = jnp.zeros(l_ref.shape, F32)
        acc_ref[...] = jnp.zeros(acc_ref.shape, F32)

    s = jnp.concatenate([_dot_nt(q_bd, pr[0, :, :D_MODEL].astype(BF16)) for pr in page_refs], axis=-1)
    s = s + add_ref[0]
    m_prev = m_ref[...]
    m_new = jnp.maximum(m_prev, jnp.max(s, axis=-1, keepdims=True))
    alpha = jnp.exp(m_prev - m_new)
    p = jnp.exp(s - m_new[:, :1]).astype(BF16)
    l_ref[...] = alpha * l_ref[...] + jnp.sum(p.astype(F32), axis=-1, keepdims=True)
    pv = _dot(p[:, :PAGE_SIZE], page_refs[0][0, :, D_MODEL:].astype(BF16))
    for i in range(1, PAGES_PER_STEP):
        pv = pv + _dot(p[:, i * PAGE_SIZE:(i + 1) * PAGE_SIZE], page_refs[i][0, :, D_MODEL:].astype(BF16))
    acc_ref[...] = alpha[:, :1] * acc_ref[...] + pv
    m_ref[...] = m_new

    @pl.when(c == pl.num_programs(1) - 1)
    def _():
        k_new = qkv_ref[0, :, D_MODEL:2 * D_MODEL].astype(BF16).astype(F32)
        v_new = qkv_ref[0, :, 2 * D_MODEL:].astype(BF16).astype(F32)
        s_new = jnp.sum(q_bd.astype(F32) * k_new, axis=-1, keepdims=True) + aux_ref[4:4 + rows, :1]
        m_prev = m_ref[...][:, :1]
        m_fin = jnp.maximum(m_prev, s_new)
        alpha = jnp.exp(m_prev - m_fin)
        p_new = jnp.exp(s_new - m_fin)
        den = alpha * l_ref[...][:, :1] + p_new
        res = (alpha * acc_ref[...] + p_new.astype(BF16).astype(F32) * v_new) / den
        if kind == "diff":
            lam = _diff_lambda(aux_ref[0:4, :HEAD_DIM], lam_init)
            o = res[:H_DIFF] - lam * res[H_DIFF:]
            r8 = lax.broadcasted_iota(jnp.int32, o.shape, 0)
            c8 = lax.broadcasted_iota(jnp.int32, o.shape, 1)
            o = jnp.where((c8 >= r8 * LANES) & (c8 < (r8 + 1) * LANES), o, 0.0)
            ms = jnp.sum(o * o, axis=-1, keepdims=True) / LANES
            o = o * lax.rsqrt(ms + RMS_EPS)
            o_ref[0] = jnp.sum(o, axis=0, keepdims=True) * g_ref[...] * (1.0 - lam_init)
        else:
            o_ref[0] = jnp.sum(jnp.where(qmask, res, 0.0), axis=0, keepdims=True)


def paged_sample(qkv, cache, page_table, add_past, aux, subln_g, kind, lam_init=0.0):
    bsz = qkv.shape[0]
    n_pages = page_table.shape[1]
    assert n_pages % PAGES_PER_STEP == 0
    n_steps = n_pages // PAGES_PER_STEP
    rows = add_past.shape[1]
    pages = cache.reshape(cache.shape[0], PAGE_SIZE, 2 * D_MODEL)
    per_seq = add_past.shape[0] == bsz

    def page_spec(i):
        return pl.BlockSpec((1, PAGE_SIZE, 2 * D_MODEL),
                            lambda b, c, pt: (pt[b * n_pages + c * PAGES_PER_STEP + i], 0, 0))

    grid_spec = pltpu.PrefetchScalarGridSpec(
        num_scalar_prefetch=1,
        grid=(bsz, n_steps),
        in_specs=[pl.BlockSpec((1, 1, 3 * D_MODEL), lambda b, c, pt: (b, 0, 0))]
                 + [page_spec(i) for i in range(PAGES_PER_STEP)]
                 + [pl.BlockSpec((1, rows, PAGES_PER_STEP * PAGE_SIZE),
                                 (lambda b, c, pt: (b, 0, c)) if per_seq else (lambda b, c, pt: (0, 0, c))),
                    pl.BlockSpec(aux.shape, lambda b, c, pt: (0, 0)),
                    pl.BlockSpec((1, D_MODEL), lambda b, c, pt: (0, 0))],
        out_specs=pl.BlockSpec((1, 1, D_MODEL), lambda b, c, pt: (b, 0, 0)),
        scratch_shapes=[pltpu.VMEM((rows, LANES), F32), pltpu.VMEM((rows, LANES), F32),
                        pltpu.VMEM((rows, D_MODEL), F32)],
    )
    out = pl.pallas_call(
        functools.partial(_paged_sample_body, kind=kind, lam_init=lam_init),
        grid_spec=grid_spec,
        out_shape=jax.ShapeDtypeStruct((bsz, 1, D_MODEL), F32),
        compiler_params=_params("parallel", "arbitrary"),
        name="paged_sample_" + kind,
    )(page_table.reshape(-1), qkv.reshape(bsz, 1, 3 * D_MODEL), *([pages] * PAGES_PER_STEP),
      add_past, aux, subln_g)
    return out.reshape(bsz, D_MODEL)


def _split3(x):
    hi = x.astype(BF16)
    r1 = x - hi.astype(F32)
    mid = r1.astype(BF16)
    lo = (r1 - mid.astype(F32)).astype(BF16)
    return hi, mid, lo


def _tri_cumsum(x, tri):
    hi, mid, lo = _split3(x)
    return _dot(tri, hi) + _dot(tri, mid) + _dot(tri, lo)


def _lower_tri(n):
    r = lax.broadcasted_iota(jnp.int32, (n, n), 0)
    c = lax.broadcasted_iota(jnp.int32, (n, n), 1)
    return jnp.where(r >= c, 1.0, 0.0).astype(BF16)


def _cumsum_body(x_ref, o_ref, carry_ref, *, chunk):
    tri = _lower_tri(chunk)
    carry_ref[...] = jnp.zeros(carry_ref.shape, F32)

    def body(i, c):
        start = pl.multiple_of(i * chunk, chunk)
        cum = _tri_cumsum(x_ref[0, pl.ds(start, chunk), :], tri) + carry_ref[...]
        o_ref[0, pl.ds(start, chunk), :] = cum
        carry_ref[...] = cum[chunk - 1:chunk, :]
        return c

    lax.fori_loop(0, x_ref.shape[1] // chunk, body, 0)


def seq_cumsum(x, chunk=128):
    bsz, seq, c = x.shape
    assert seq % chunk == 0
    return pl.pallas_call(
        functools.partial(_cumsum_body, chunk=chunk),
        grid=(bsz,),
        in_specs=[pl.BlockSpec((1, seq, c), lambda b: (b, 0, 0))],
        out_specs=pl.BlockSpec((1, seq, c), lambda b: (b, 0, 0)),
        out_shape=jax.ShapeDtypeStruct((bsz, seq, c), F32),
        scratch_shapes=[pltpu.VMEM((1, c), F32)],
        compiler_params=_params("parallel"),
        name="seq_cumsum",
    )(x)


def _fox_prompt_body(q_ref, k_ref, v_ref, ccol_ref, crow_ref, o_ref, m_ref, l_ref, acc_ref, *, t):
    qi = pl.program_id(2)
    q = q_ref[...]
    ones = _half_ones(q.shape)
    qs = [q * one for one in ones]
    m_ref[...] = jnp.full(m_ref.shape, NEG, F32)
    l_ref[...] = jnp.zeros(l_ref.shape, F32)
    acc_ref[...] = jnp.zeros(acc_ref.shape, F32)
    cq = ccol_ref[0, 0]
    cref = [cq[0:1, s:s + 1] for s in range(2)]
    a_col = [cq[:, s:s + 1] - cref[s] for s in range(2)]

    def tile(kj, diag):
        start = pl.multiple_of(kj * t, t)
        k = k_ref[pl.ds(start, t), :]
        v = v_ref[pl.ds(start, t), :]
        crow = crow_ref[0, 0, :, pl.ds(start, t)]
        for slot in range(2):
            s = _dot_nt(qs[slot], k) + a_col[slot] - (crow[slot:slot + 1, :] - cref[slot])
            if diag:
                r = lax.broadcasted_iota(jnp.int32, s.shape, 0)
                c = lax.broadcasted_iota(jnp.int32, s.shape, 1)
                s = jnp.where(r >= c, s, NEG)
            m_prev = m_ref[slot]
            m_new = jnp.maximum(m_prev, jnp.max(s, axis=-1, keepdims=True))
            alpha = jnp.exp(m_prev - m_new)
            p = jnp.exp(s - m_new[:, :1])
            l_ref[slot] = alpha * l_ref[slot] + jnp.sum(p, axis=-1, keepdims=True)
            acc_ref[slot] = alpha * acc_ref[slot] + _dot(p.astype(BF16), v * ones[slot])
            m_ref[slot] = m_new

    def body(kj, c):
        tile(kj, False)
        return c

    lax.fori_loop(0, qi, body, 0)
    tile(qi, True)
    o_ref[...] = (acc_ref[0] / l_ref[0] + acc_ref[1] / l_ref[1]).astype(o_ref.dtype)


def fox_prompt(q, kv, cum, bsz, seq, t=256):
    nq = seq // t
    n_pair = N_HEADS // 2
    ccol = cum.reshape(bsz, seq, n_pair, 2).transpose(0, 2, 1, 3)
    crow = cum.reshape(bsz, seq, n_pair, 2).transpose(0, 2, 3, 1)
    return pl.pallas_call(
        functools.partial(_fox_prompt_body, t=t),
        grid=(bsz, n_pair, nq),
        in_specs=[pl.BlockSpec((t, LANES), lambda b, h, i: (b * nq + i, h)),
                  pl.BlockSpec((seq, LANES), lambda b, h, i: (b, h)),
                  pl.BlockSpec((seq, LANES), lambda b, h, i: (b, n_pair + h)),
                  pl.BlockSpec((1, 1, t, 2), lambda b, h, i: (b, h, i, 0)),
                  pl.BlockSpec((1, 1, 2, seq), lambda b, h, i: (b, h, 0, 0))],
        out_specs=pl.BlockSpec((t, LANES), lambda b, h, i: (b * nq + i, h)),
        out_shape=jax.ShapeDtypeStruct((bsz * seq, D_MODEL), BF16),
        scratch_shapes=[pltpu.VMEM((2, t, LANES), F32), pltpu.VMEM((2, t, LANES), F32),
                        pltpu.VMEM((2, t, LANES), F32)],
        compiler_params=_params("parallel", "parallel", "arbitrary"),
        name="fox_prompt",
    )(q, kv, kv, ccol, crow)


def _fox_decay_body(pt_ref, *refs):
    n_pages = len(refs) - 3
    page_refs, new_ref, o_ref, carry_ref = refs[:n_pages], refs[n_pages], refs[n_pages + 1], refs[n_pages + 2]
    tri = _lower_tri(PAGE_SIZE)
    carry_ref[...] = new_ref[0]
    for i in reversed(range(n_pages)):
        x = page_refs[i][0]
        cum = _tri_cumsum(x, tri)
        total = cum[PAGE_SIZE - 1:PAGE_SIZE, :]
        o_ref[0, i * PAGE_SIZE:(i + 1) * PAGE_SIZE, :] = (total - cum) + carry_ref[...]
        carry_ref[...] = carry_ref[...] + total


def fox_sample_decay(cache_logf, page_table, logf_new):
    bsz, n_pages = page_table.shape
    nh = cache_logf.shape[-1]

    def page_spec(i):
        return pl.BlockSpec((1, PAGE_SIZE, nh), lambda b, pt: (pt[b * n_pages + i], 0, 0))

    grid_spec = pltpu.PrefetchScalarGridSpec(
        num_scalar_prefetch=1,
        grid=(bsz,),
        in_specs=[page_spec(i) for i in range(n_pages)] + [pl.BlockSpec((1, 1, nh), lambda b, pt: (b, 0, 0))],
        out_specs=pl.BlockSpec((1, n_pages * PAGE_SIZE, nh), lambda b, pt: (b, 0, 0)),
        scratch_shapes=[pltpu.VMEM((1, nh), F32)],
    )
    return pl.pallas_call(
        _fox_decay_body,
        grid_spec=grid_spec,
        out_shape=jax.ShapeDtypeStruct((bsz, n_pages * PAGE_SIZE, nh), F32),
        compiler_params=_params("parallel"),
        name="fox_sample_decay",
    )(page_table.reshape(-1), *([cache_logf] * n_pages), logf_new.reshape(bsz, 1, nh))


def _pad_cols(w, n):
    return jnp.pad(w, [(0, 0)] * (w.ndim - 1) + [(0, n - w.shape[-1])])


def _split_gu(w_gu):
    gate, up = w_gu[..., :D_FF], w_gu[..., D_FF:]
    return _pad_cols(gate, D_FF_PAD).astype(BF16), _pad_cols(up, D_FF_PAD).astype(BF16)


def _pad_down(w_down):
    pad = [(0, 0)] * (w_down.ndim - 2) + [(0, D_FF_PAD - D_FF), (0, 0)]
    return jnp.pad(w_down, pad).astype(BF16)


def _mixer_a(hp, hs, bufs, w_qkv, w_o, ln_g, ln_b, bias_dist, bsz, seq):
    w = w_qkv.astype(BF16)
    col = np.arange(w.shape[1]) % (3 * D_MODEL) < D_MODEL
    wq_scaled = jnp.where(jnp.asarray(col)[None, :], w * ATTN_SCALE, w)
    (qkv_p,) = linear(hp, wq_scaled, [BF16])
    (qkv_s,) = linear(hs, w, [F32])

    tail = min(DIL_GROUPS[-1][0], seq)
    kv_cols = np.concatenate([np.arange(g * 3 * D_MODEL + D_MODEL, (g + 1) * 3 * D_MODEL) for g in range(3)])
    x_tail = hp.reshape(bsz, seq, D_MODEL)[:, seq - tail:].reshape(bsz * tail, D_MODEL)
    (kv_tail,) = linear(x_tail, w[:, kv_cols], [F32])
    kv_tail = kv_tail.reshape(bsz, tail, 3, 2, N_HEADS, HEAD_DIM)
    bufs_p = tuple(kv_tail[:, tail - min(wd, seq):, g] for g, (wd, _) in enumerate(DIL_GROUPS))

    tiles = _dil_bias_tiles(bias_dist)
    outs, lses = [], []
    for g, (_, dil) in enumerate(DIL_GROUPS):
        o, l = dilated_prompt_group(qkv_p, tiles[g], g, dil, bsz, seq)
        outs.append(o)
        lses.append(l)
    hp = merge_o_ln(outs, lses, w_o, hp, ln_g, ln_b)

    bias_buf, bias_self = _dil_sample_bias(bias_dist)
    ys = dilated_sample(qkv_s, bufs, bias_buf, bias_self)
    hs = linear_ln(ys, w_o, hs, ln_g, ln_b)
    dec = hs.shape[0]
    new_kv = qkv_s.reshape(dec, 1, 3, 3, N_HEADS, HEAD_DIM)[:, :, :, 1:3]
    bufs_s = tuple(jnp.concatenate([buf[:, 1:], new_kv[:, :, g]], axis=1) for g, buf in enumerate(bufs))
    return hp, hs, bufs_p, bufs_s


def _mixer_b(hp, hs, cache_kv, page_table, w_qkv, lam_params, subln_g, w_o, ln_g, ln_b, bias_dist,
             layer_idx, bsz, seq):
    w = w_qkv.astype(BF16)
    lam_init = 0.8 - 0.6 * math.exp(-0.3 * layer_idx)
    (q_p,) = linear(hp, w[:, :D_MODEL], [BF16], scale=ATTN_SCALE)
    kv_p, kv_pb = linear(hp, w[:, D_MODEL:], [F32, BF16])
    (qkv_s,) = linear(hs, w, [F32])
    t = 256
    o = diff_prompt(q_p, kv_pb, _diff_bias_tiles(bias_dist, t), lam_params, subln_g, lam_init, bsz, seq, t)
    hp = linear_ln(o, w_o, hp, ln_g, ln_b)

    dec = hs.shape[0]
    n_past = page_table.shape[1] * PAGE_SIZE
    order = np.array([2 * (r % H_DIFF) + r // H_DIFF for r in range(2 * H_DIFF)])
    bias_rows = bias_dist[order]
    dist = n_past - np.arange(n_past)
    add_past = jnp.take(bias_rows, jnp.asarray(dist, jnp.int32), axis=1)[None]
    aux = jnp.zeros((4 + 2 * H_DIFF + 4, LANES), F32)
    aux = aux.at[0:4, :HEAD_DIM].set(lam_params.astype(F32))
    aux = aux.at[4:4 + 2 * H_DIFF, :].set(jnp.broadcast_to(bias_rows[:, :1], (2 * H_DIFF, LANES)))
    ys = paged_sample(qkv_s, cache_kv, page_table, add_past, aux,
                      jnp.tile(subln_g.astype(F32), H_DIFF).reshape(1, D_MODEL), "diff", lam_init)
    hs = linear_ln(ys, w_o, hs, ln_g, ln_b)
    st_p = kv_p.reshape(bsz, seq, 2, H_DIFF, 2 * HEAD_DIM)
    st_s = qkv_s[:, D_MODEL:].reshape(dec, 1, 2, H_DIFF, 2 * HEAD_DIM)
    return hp, hs, st_p, st_s


def _mixer_c(hp, hs, cache_kv, cache_logf, page_table, w_qkv, w_f, b_f, w_o, ln_g, ln_b, bsz, seq):
    w = w_qkv.astype(BF16)
    wf = _pad_cols(w_f, LANES).astype(BF16)
    bf = _pad_cols(b_f.reshape(1, N_HEADS), LANES).astype(F32)
    (q_p,) = linear(hp, w[:, :D_MODEL], [BF16], scale=ATTN_SCALE)
    kv_p, kv_pb = linear(hp, w[:, D_MODEL:], [F32, BF16])
    (qkv_s,) = linear(hs, w, [F32])
    logf_p = forget_log(hp, wf, bf)[:, :N_HEADS].reshape(bsz, seq, N_HEADS)
    logf_s = forget_log(hs, wf, bf)[:, :N_HEADS]
    cum = seq_cumsum(logf_p)
    o = fox_prompt(q_p, kv_pb, cum, bsz, seq)
    hp = linear_ln(o, w_o, hp, ln_g, ln_b)

    dec = hs.shape[0]
    decay = fox_sample_decay(cache_logf, page_table, logf_s)
    add_past = decay.transpose(0, 2, 1)
    aux = jnp.zeros((4 + N_HEADS + 4, LANES), F32)
    ys = paged_sample(qkv_s, cache_kv, page_table, add_past, aux, jnp.ones((1, D_MODEL), F32), "fox")
    hs = linear_ln(ys, w_o, hs, ln_g, ln_b)
    st_p = (kv_p.reshape(bsz, seq, 2, N_HEADS, HEAD_DIM), logf_p)
    st_s = (qkv_s[:, D_MODEL:].reshape(dec, 1, 2, N_HEADS, HEAD_DIM), logf_s.reshape(dec, 1, N_HEADS))
    return hp, hs, st_p, st_s


def _dense_ffn(h, w_gu, w_down, ln_g, ln_b):
    gate, up = _split_gu(w_gu)
    return ffn_ln(h, gate[None], up[None], _pad_down(w_down)[None], None, ln_g, ln_b)


def _moe_ffn(h, w_router, w_gu, w_down, ln_g, ln_b):
    combine = moe_router(h, _pad_cols(w_router, LANES).astype(BF16))
    gate, up = _split_gu(w_gu)
    return ffn_ln(h, gate, up, _pad_down(w_down), combine, ln_g, ln_b)


def kernel(x_prompt, x_sample, cache_l0_kv_w128, cache_l0_kv_w512, cache_l0_kv_w2048, cache_l1_kv, cache_l2_kv, cache_l2_logf, cache_l3_kv_w128, cache_l3_kv_w512, cache_l3_kv_w2048, page_table, p_prompt, p_sample, rel_bias, ln_g, ln_b, ple_w_in, ple_w_gate, ple_b_gate, a_w_qkv, a_w_o, b_w_qkv, b_lambda, b_subln_g, b_w_o, c_w_qkv, c_w_f, c_b_f, c_w_o, ffn_w_gu, ffn_w_down, moe_router, moe_w_gu, moe_w_down):
    bsz, seq, d = x_prompt.shape
    dec = x_sample.shape[0]
    assert d == D_MODEL and x_sample.shape[1] == 1 and seq % DIL_GROUPS[-1][0] == 0
    caches = ((cache_l0_kv_w128, cache_l0_kv_w512, cache_l0_kv_w2048), (cache_l1_kv,),
              (cache_l2_kv, cache_l2_logf), (cache_l3_kv_w128, cache_l3_kv_w512, cache_l3_kv_w2048))
    bias_dist = _bias_by_distance(rel_bias, max(MAX_DISTANCE, page_table.shape[1] * PAGE_SIZE))
    hp = x_prompt.reshape(bsz * seq, d)
    hs = x_sample.reshape(dec, d)
    states = []
    for i in range(DEPTH):
        kind, j = i % 3, i // 3
        g0, b0, g1, b1 = ln_g[i, 0], ln_b[i, 0], ln_g[i, 1], ln_b[i, 1]
        if kind == 0:
            hp, hs, st_p, st_s = _mixer_a(hp, hs, caches[i], a_w_qkv[j], a_w_o[j].astype(BF16), g0, b0,
                                          bias_dist, bsz, seq)
        elif kind == 1:
            hp, hs, st_p, st_s = _mixer_b(hp, hs, caches[i][0], page_table, b_w_qkv[j], b_lambda[j],
                                          b_subln_g[j], b_w_o[j].astype(BF16), g0, b0, bias_dist, i, bsz, seq)
            st_p, st_s = (st_p,), (st_s,)
        else:
            hp, hs, st_p, st_s = _mixer_c(hp, hs, caches[i][0], caches[i][1], page_table, c_w_qkv[j],
                                          c_w_f[j], c_b_f[j], c_w_o[j].astype(BF16), g0, b0, bsz, seq)
        states.append((st_p, st_s))
        f = i // 2
        if i % 2 == 0:
            hp = _dense_ffn(hp, ffn_w_gu[f], ffn_w_down[f], g1, b1)
            hs = _dense_ffn(hs, ffn_w_gu[f], ffn_w_down[f], g1, b1)
        else:
            hp = _moe_ffn(hp, moe_router[f], moe_w_gu[f], moe_w_down[f], g1, b1)
            hs = _moe_ffn(hs, moe_router[f], moe_w_gu[f], moe_w_down[f], g1, b1)
        wg, wi = ple_w_gate[i].astype(BF16), ple_w_in[i].astype(BF16)
        hp = per_layer_embed(hp, p_prompt[i].reshape(bsz * seq, D_PLE), wg, ple_b_gate[i], wi)
        hs = per_layer_embed(hs, p_sample[i].reshape(dec, D_PLE), wg, ple_b_gate[i], wi)
    (l0_p, l0_s), (l1_p, l1_s), (l2_p, l2_s), (l3_p, l3_s) = states
    return (hp.reshape(bsz, seq, d), hs.reshape(dec, 1, d),
            l0_p[0], l0_s[0], l0_p[1], l0_s[1], l0_p[2], l0_s[2],
            l1_p[0], l1_s[0],
            l2_p[0], l2_s[0], l2_p[1], l2_s[1],
            l3_p[0], l3_s[0], l3_p[1], l3_s[1], l3_p[2], l3_s[2])
```

```python
import functools
import math

import numpy as np
import jax
import jax.numpy as jnp
from jax import lax
from jax.experimental import pallas as pl
from jax.experimental.pallas import tpu as pltpu

F32 = jnp.float32
BF16 = jnp.bfloat16

D_MODEL = 1024
HEAD_DIM = 64
N_HEADS = 16
H_DIFF = 8
DIL_GROUPS = ((128, 1), (512, 4), (2048, 16))
N_KEYS = 128
N_BUCKETS = 32
MAX_DISTANCE = 2048
D_FF = 2752
D_FF_PAD = 2816
N_EXPERTS = 8
D_PLE = 256
DEPTH = 4
PAGE_SIZE = 128
LN_EPS = 1e-5
RMS_EPS = 1e-5
ALPHA = (2 * DEPTH) ** 0.25
ATTN_SCALE = HEAD_DIM ** -0.5
NEG = -1e30
LANES = 128
FAR_DIST = 1520
VMEM_LIMIT = 56 * 1024 * 1024


def _params(*sem):
    return pltpu.CompilerParams(dimension_semantics=sem, vmem_limit_bytes=VMEM_LIMIT)


def _dot(a, b):
    return jnp.dot(a, b, preferred_element_type=F32)


def _dot_nt(a, b):
    return lax.dot_general(a, b, (((1,), (1,)), ((), ())), preferred_element_type=F32)


def _layer_norm(y, g, b):
    mu = jnp.mean(y, axis=-1, keepdims=True)
    yc = y - mu
    var = jnp.mean(yc * yc, axis=-1, keepdims=True)
    return yc * lax.rsqrt(var + LN_EPS) * g + b


def _sigmoid(x):
    return 1.0 / (1.0 + jnp.exp(-x))


def _linear_body(x_ref, w_ref, *o_refs, scale):
    acc = _dot(x_ref[...].astype(BF16), w_ref[...])
    if scale is not None:
        acc = acc * scale
    for o_ref in o_refs:
        o_ref[...] = acc.astype(o_ref.dtype)


def linear(x, w, out_dtypes, scale=None, tm=512, tn=1024):
    m, k = x.shape
    n = w.shape[1]
    tm, tn = min(tm, m), min(tn, n)
    assert m % tm == 0 and n % tn == 0
    outs = pl.pallas_call(
        functools.partial(_linear_body, scale=scale),
        grid=(m // tm, n // tn),
        in_specs=[pl.BlockSpec((tm, k), lambda i, j: (i, 0)),
                  pl.BlockSpec((k, tn), lambda i, j: (0, j))],
        out_specs=[pl.BlockSpec((tm, tn), lambda i, j: (i, j)) for _ in out_dtypes],
        out_shape=[jax.ShapeDtypeStruct((m, n), d) for d in out_dtypes],
        compiler_params=_params("parallel", "parallel"),
        name="linear",
    )(x, w)
    return outs


def _forget_body(x_ref, w_ref, b_ref, o_ref):
    z = _dot(x_ref[...].astype(BF16), w_ref[...]) + b_ref[...]
    o_ref[...] = -(jnp.maximum(-z, 0.0) + jnp.log1p(jnp.exp(-jnp.abs(z))))


def forget_log(x, w_pad, b_pad, tm=512):
    m, k = x.shape
    tm = min(tm, m)
    assert m % tm == 0
    return pl.pallas_call(
        _forget_body,
        grid=(m // tm,),
        in_specs=[pl.BlockSpec((tm, k), lambda i: (i, 0)),
                  pl.BlockSpec((k, LANES), lambda i: (0, 0)),
                  pl.BlockSpec((1, LANES), lambda i: (0, 0))],
        out_specs=pl.BlockSpec((tm, LANES), lambda i: (i, 0)),
        out_shape=jax.ShapeDtypeStruct((m, LANES), F32),
        compiler_params=_params("parallel"),
        name="forget_log",
    )(x, w_pad, b_pad)


def _linear_ln_body(a_ref, w_ref, r_ref, g_ref, b_ref, o_ref):
    sub = _dot(a_ref[...].astype(BF16), w_ref[...])
    o_ref[...] = _layer_norm(ALPHA * r_ref[...] + sub, g_ref[...], b_ref[...])


def linear_ln(a, w, resid, g, b, tm=512):
    m, k = a.shape
    n = w.shape[1]
    tm = min(tm, m)
    assert m % tm == 0
    return pl.pallas_call(
        _linear_ln_body,
        grid=(m // tm,),
        in_specs=[pl.BlockSpec((tm, k), lambda i: (i, 0)),
                  pl.BlockSpec((k, n), lambda i: (0, 0)),
                  pl.BlockSpec((tm, n), lambda i: (i, 0)),
                  pl.BlockSpec((1, n), lambda i: (0, 0)),
                  pl.BlockSpec((1, n), lambda i: (0, 0))],
        out_specs=pl.BlockSpec((tm, n), lambda i: (i, 0)),
        out_shape=jax.ShapeDtypeStruct((m, n), F32),
        compiler_params=_params("parallel"),
        name="linear_ln",
    )(a, w, resid, g.reshape(1, n), b.reshape(1, n))


def _ple_body(x_ref, p_ref, wg_ref, bg_ref, wp_ref, o_ref):
    x = x_ref[...]
    gate = _sigmoid(_dot(x.astype(BF16), wg_ref[...]) + bg_ref[...])
    o_ref[...] = x + gate * _dot(p_ref[...].astype(BF16), wp_ref[...])


def per_layer_embed(x, p, w_gate, b_gate, w_in, tm=512):
    m, d = x.shape
    dp = p.shape[1]
    tm = min(tm, m)
    assert m % tm == 0
    return pl.pallas_call(
        _ple_body,
        grid=(m // tm,),
        in_specs=[pl.BlockSpec((tm, d), lambda i: (i, 0)),
                  pl.BlockSpec((tm, dp), lambda i: (i, 0)),
                  pl.BlockSpec((d, d), lambda i: (0, 0)),
                  pl.BlockSpec((1, d), lambda i: (0, 0)),
                  pl.BlockSpec((dp, d), lambda i: (0, 0))],
        out_specs=pl.BlockSpec((tm, d), lambda i: (i, 0)),
        out_shape=jax.ShapeDtypeStruct((m, d), F32),
        compiler_params=_params("parallel"),
        name="per_layer_embed",
    )(x, p, w_gate, b_gate.reshape(1, d), w_in)


def _router_body(x_ref, w_ref, o_ref):
    logits = _dot(x_ref[...].astype(BF16), w_ref[...])
    lane = lax.broadcasted_iota(jnp.int32, logits.shape, 1)
    logits = jnp.where(lane < N_EXPERTS, logits, NEG)
    v1 = jnp.max(logits, axis=-1, keepdims=True)
    i1 = jnp.min(jnp.where(logits == v1, lane, LANES), axis=-1, keepdims=True)
    rest = jnp.where(lane == i1, NEG, logits)
    v2 = jnp.max(rest, axis=-1, keepdims=True)
    i2 = jnp.min(jnp.where(rest == v2, lane, LANES), axis=-1, keepdims=True)
    e2 = jnp.exp(v2 - v1)
    g1 = 1.0 / (1.0 + e2)
    g2 = e2 / (1.0 + e2)
    o_ref[...] = jnp.where(lane == i1, g1, 0.0) + jnp.where(lane == i2, g2, 0.0)


def moe_router(x, w_pad, tm=512):
    m, k = x.shape
    tm = min(tm, m)
    assert m % tm == 0
    return pl.pallas_call(
        _router_body,
        grid=(m // tm,),
        in_specs=[pl.BlockSpec((tm, k), lambda i: (i, 0)),
                  pl.BlockSpec((k, LANES), lambda i: (0, 0))],
        out_specs=pl.BlockSpec((tm, LANES), lambda i: (i, 0)),
        out_shape=jax.ShapeDtypeStruct((m, LANES), F32),
        compiler_params=_params("parallel"),
        name="moe_router",
    )(x, w_pad)


def _ffn_body(x_ref, wg_ref, wu_ref, wd_ref, c_ref, g_ref, b_ref, o_ref, xb_ref, acc_ref, *, gated):
    e, f = pl.program_id(1), pl.program_id(2)

    @pl.when((e == 0) & (f == 0))
    def _():
        xb_ref[...] = x_ref[...].astype(BF16)
        acc_ref[...] = jnp.zeros_like(acc_ref)

    xb = xb_ref[...]
    gate = _dot(xb, wg_ref[0])
    up = _dot(xb, wu_ref[0])
    h = (gate * _sigmoid(gate) * up).astype(BF16)
    y = _dot(h, wd_ref[0])
    if gated:
        c = c_ref[...]
        lane = lax.broadcasted_iota(jnp.int32, c.shape, 1)
        y = y * jnp.sum(jnp.where(lane == e, c, 0.0), axis=-1, keepdims=True)
    acc_ref[...] += y

    @pl.when((e == pl.num_programs(1) - 1) & (f == pl.num_programs(2) - 1))
    def _():
        o_ref[...] = _layer_norm(ALPHA * x_ref[...] + acc_ref[...], g_ref[...], b_ref[...])


def ffn_ln(x, w_gate, w_up, w_down, combine, g, b, tm=1024, tf=256):
    m, d = x.shape
    n_e, _, f_pad = w_gate.shape
    tm = min(tm, m)
    assert m % tm == 0 and f_pad % tf == 0
    gated = combine is not None
    if not gated:
        combine = jnp.zeros((m, LANES), F32)
    return pl.pallas_call(
        functools.partial(_ffn_body, gated=gated),
        grid=(m // tm, n_e, f_pad // tf),
        in_specs=[pl.BlockSpec((tm, d), lambda i, e, f: (i, 0)),
                  pl.BlockSpec((1, d, tf), lambda i, e, f: (e, 0, f)),
                  pl.BlockSpec((1, d, tf), lambda i, e, f: (e, 0, f)),
                  pl.BlockSpec((1, tf, d), lambda i, e, f: (e, f, 0)),
                  pl.BlockSpec((tm, LANES), lambda i, e, f: (i, 0)),
                  pl.BlockSpec((1, d), lambda i, e, f: (0, 0)),
                  pl.BlockSpec((1, d), lambda i, e, f: (0, 0))],
        out_specs=pl.BlockSpec((tm, d), lambda i, e, f: (i, 0)),
        out_shape=jax.ShapeDtypeStruct((m, d), F32),
        scratch_shapes=[pltpu.VMEM((tm, d), BF16), pltpu.VMEM((tm, d), F32)],
        compiler_params=_params("parallel", "arbitrary", "arbitrary"),
        name="ffn_ln",
    )(x, w_gate, w_up, w_down, combine, g.reshape(1, d), b.reshape(1, d))


def _rel_bucket(dist):
    dist = jnp.maximum(dist, 0)
    max_exact = N_BUCKETS // 2
    df = jnp.maximum(dist, 1).astype(F32)
    large = max_exact + (jnp.log(df / max_exact) / math.log(MAX_DISTANCE / max_exact)
                         * (N_BUCKETS - max_exact)).astype(jnp.int32)
    large = jnp.clip(large, 0, N_BUCKETS - 1)
    return jnp.where(dist < max_exact, dist, large)


def _bias_by_distance(rel_bias, max_dist):
    return rel_bias[_rel_bucket(jnp.arange(max_dist + 1))].T.astype(F32)


def _half_masks(shape):
    lane = lax.broadcasted_iota(jnp.int32, shape, 1)
    return lane < HEAD_DIM, lane >= HEAD_DIM


def _half_ones(shape):
    lo, hi = _half_masks(shape)
    return jnp.where(lo, 1.0, 0.0).astype(BF16), jnp.where(hi, 1.0, 0.0).astype(BF16)


def _block_diag_rows(q_row, mask):
    return jnp.where(mask, jnp.broadcast_to(q_row, mask.shape), 0.0).astype(BF16)


def _dil_prompt_body(q_ref, kp_ref, kc_ref, vp_ref, vc_ref, bias_ref, o_ref, lse_ref):
    n = N_KEYS
    first = pl.program_id(2) == 0
    pen = jnp.where(first, NEG, 0.0).astype(F32)
    masks = _half_masks((n, LANES))
    ones = _half_ones((n, LANES))
    for pr in range(N_HEADS // 2):
        sl = slice(pr * LANES, (pr + 1) * LANES)
        q, kp, kc, vp, vc = q_ref[:, sl], kp_ref[:, sl], kc_ref[:, sl], vp_ref[:, sl], vc_ref[:, sl]
        o_pair = jnp.zeros((n, LANES), F32)
        lse_pair = jnp.zeros((n, LANES), F32)
        for half, (msk, one) in enumerate(zip(masks, ones)):
            h = 2 * pr + half
            qm = q * one
            s_prev = _dot_nt(qm, kp) + bias_ref[h, :, :n] + pen
            s_cur = _dot_nt(qm, kc) + bias_ref[h, :, n:]
            mx = jnp.maximum(jnp.max(s_prev, axis=-1, keepdims=True), jnp.max(s_cur, axis=-1, keepdims=True))
            p_prev = jnp.exp(s_prev - mx)
            p_cur = jnp.exp(s_cur - mx)
            den = jnp.sum(p_prev, axis=-1, keepdims=True) + jnp.sum(p_cur, axis=-1, keepdims=True)
            acc = _dot(p_prev.astype(BF16), vp * one) + _dot(p_cur.astype(BF16), vc * one)
            o_pair = o_pair + acc / den
            lse_pair = jnp.where(msk, mx + jnp.log(den), lse_pair)
        o_ref[:, sl] = o_pair
        lse_ref[:, sl] = lse_pair


def _linear_dil_body(x_ref, w_ref, o_ref, sc_ref, *, dil):
    acc = _dot(x_ref[...].astype(BF16), w_ref[...])
    rows = acc.shape[0] // dil
    for cb in range(acc.shape[1] // LANES):
        lanes = slice(cb * LANES, (cb + 1) * LANES)
        sc_ref[cb] = acc[:, lanes]
        for r in range(dil):
            o_ref[0, r, :, lanes] = sc_ref[cb, pl.ds(r, rows, stride=dil), :].astype(o_ref.dtype)


def linear_dil(x, w, dil, bsz, seq, tm=512, tn=1024):
    m, k = x.shape
    n = w.shape[1]
    tm = min(tm, seq)
    assert seq % tm == 0 and n % tn == 0 and tm % (16 * dil) == 0
    nbs = seq // tm
    return pl.pallas_call(
        functools.partial(_linear_dil_body, dil=dil),
        grid=(m // tm, n // tn),
        in_specs=[pl.BlockSpec((tm, k), lambda i, j: (i, 0)),
                  pl.BlockSpec((k, tn), lambda i, j: (0, j))],
        out_specs=pl.BlockSpec((1, dil, tm // dil, tn), lambda i, j: (i // nbs, 0, i % nbs, j)),
        out_shape=jax.ShapeDtypeStruct((bsz, dil, seq // dil, n), BF16),
        scratch_shapes=[pltpu.VMEM((tn // LANES, tm, LANES), F32)],
        compiler_params=_params("parallel", "parallel"),
        name="linear_dil",
    )(x, w)


def dilated_prompt_group(qkv, bias_tile, dil, bsz, seq):
    n = N_KEYS
    nb = seq // (dil * n)
    view = qkv.reshape(bsz * seq, 3 * D_MODEL)

    def cur(which):
        return lambda b, r, i: ((b * dil + r) * nb + i, which)

    def prev(which):
        return lambda b, r, i: ((b * dil + r) * nb + jnp.maximum(i - 1, 0), which)

    blk = (n, D_MODEL)
    o, lse = pl.pallas_call(
        _dil_prompt_body,
        grid=(bsz, dil, nb),
        in_specs=[pl.BlockSpec(blk, cur(0)),
                  pl.BlockSpec(blk, prev(1)), pl.BlockSpec(blk, cur(1)),
                  pl.BlockSpec(blk, prev(2)), pl.BlockSpec(blk, cur(2)),
                  pl.BlockSpec((N_HEADS, n, 2 * n), lambda b, r, i: (0, 0, 0))],
        out_specs=[pl.BlockSpec(blk, cur(0)), pl.BlockSpec(blk, cur(0))],
        out_shape=[jax.ShapeDtypeStruct((bsz * seq, D_MODEL), F32)] * 2,
        compiler_params=_params("parallel", "parallel", "arbitrary"),
        name="dilated_prompt_group",
    )(view, view, view, view, view, bias_tile)
    shape = (bsz, dil, seq // dil, D_MODEL)
    return o.reshape(shape), lse.reshape(shape)


def _merge_o_ln_body(o1, o2, o3, l1, l2, l3, w_ref, r_ref, g_ref, b_ref, out_ref, nat_ref):
    tm = out_ref.shape[0]
    ys = []
    for cb in range(D_MODEL // LANES):
        lanes = slice(cb * LANES, (cb + 1) * LANES)
        vals = []
        for j, src in enumerate((o1, o2, o3, l1, l2, l3)):
            dil = src.shape[1]
            if dil == 1:
                vals.append(src[0, 0, :, lanes])
                continue
            for r in range(dil):
                nat_ref[j, pl.ds(r, tm // dil, stride=dil), :] = src[0, r, :, lanes]
            vals.append(nat_ref[j])
        a, b, c = vals[3:]
        mx = jnp.maximum(jnp.maximum(a, b), c)
        ea, eb, ec = jnp.exp(a - mx), jnp.exp(b - mx), jnp.exp(c - mx)
        ys.append(((ea * vals[0] + eb * vals[1] + ec * vals[2]) / (ea + eb + ec)).astype(BF16))
    sub = _dot(jnp.concatenate(ys, axis=1), w_ref[...])
    out_ref[...] = _layer_norm(ALPHA * r_ref[...] + sub, g_ref[...], b_ref[...])


def merge_o_ln(outs, lses, w_o, resid, g, b, bsz, seq, tm=256):
    m, d = resid.shape
    tm = min(tm, seq)
    assert seq % tm == 0
    nbs = seq // tm
    row = pl.BlockSpec((tm, d), lambda i: (i, 0))
    vec = pl.BlockSpec((1, d), lambda i: (0, 0))

    def grp(a):
        dil = a.shape[1]
        assert tm % (8 * dil) == 0
        return pl.BlockSpec((1, dil, tm // dil, d), lambda i: (i // nbs, 0, i % nbs, 0))

    return pl.pallas_call(
        _merge_o_ln_body,
        grid=(m // tm,),
        in_specs=[grp(a) for a in (*outs, *lses)] + [pl.BlockSpec((d, d), lambda i: (0, 0)), row, vec, vec],
        out_specs=row,
        out_shape=jax.ShapeDtypeStruct((m, d), F32),
        scratch_shapes=[pltpu.VMEM((6, tm, LANES), F32)],
        compiler_params=_params("parallel"),
        name="merge_o_ln",
    )(*outs, *lses, w_o, resid, g.reshape(1, d), b.reshape(1, d))


def _dil_bias_tiles(bias_dist):
    n = N_KEYS
    step = n + np.arange(n)[:, None] - np.arange(2 * n)[None, :]
    band = (step >= 0) & (step <= n)
    tiles = []
    for _, dil in DIL_GROUPS:
        idx = np.clip(step, 0, n) * dil
        t = jnp.take(bias_dist, jnp.asarray(idx.reshape(-1), jnp.int32), axis=1).reshape(N_HEADS, n, 2 * n)
        tiles.append(jnp.where(jnp.asarray(band)[None], t, NEG))
    return tiles


def _block_diag_mask(rows, width):
    r = lax.broadcasted_iota(jnp.int32, (rows, rows * width), 0)
    c = lax.broadcasted_iota(jnp.int32, (rows, rows * width), 1)
    return (c >= r * width) & (c < (r + 1) * width)


SAMPLE_HEADS_PER_STEP = 4


def _bf16_round(x):
    return x.astype(BF16).astype(F32)


def _dil_sample_body(q_ref, new_ref, b1_ref, b2_ref, b3_ref, a1_ref, a2_ref, a3_ref, a0_ref,
                     y_ref, o1_ref, o2_ref, o3_ref):
    for h in range(SAMPLE_HEADS_PER_STEP):
        outs, lses = [], []
        for g, (buf_ref, add_ref, out_ref) in enumerate(((b1_ref, a1_ref, o1_ref), (b2_ref, a2_ref, o2_ref),
                                                         (b3_ref, a3_ref, o3_ref))):
            w = buf_ref.shape[-1]
            q = _bf16_round(q_ref[0, g, h] * ATTN_SCALE)
            k_new, v_new = new_ref[0, g, 0, h], new_ref[0, g, 1, h]
            k_buf, v_buf = buf_ref[0, 0, h], buf_ref[0, 1, h]
            s = jnp.sum(_bf16_round(k_buf) * q, axis=0, keepdims=True) + add_ref[0, h:h + 1, :]
            s_new = jnp.sum(_bf16_round(k_new) * q, axis=0, keepdims=True) + a0_ref[0, h:h + 1, :1]
            mx = jnp.maximum(jnp.max(s, axis=1, keepdims=True), s_new)
            p = jnp.exp(s - mx)
            p_new = jnp.exp(s_new - mx)
            den = jnp.sum(p, axis=1, keepdims=True) + p_new
            acc = (jnp.sum(_bf16_round(v_buf) * _bf16_round(p), axis=1, keepdims=True)
                   + _bf16_round(p_new) * _bf16_round(v_new))
            outs.append(acc / den)
            lses.append(mx + jnp.log(den))
            last = lax.broadcasted_iota(jnp.int32, (HEAD_DIM, w), 1) == w - 1
            out_ref[0, 0, h] = jnp.where(last, k_new, pltpu.roll(k_buf, w - 1, 1))
            out_ref[0, 1, h] = jnp.where(last, v_new, pltpu.roll(v_buf, w - 1, 1))
        mx = jnp.maximum(jnp.maximum(lses[0], lses[1]), lses[2])
        ws = [jnp.exp(l - mx) for l in lses]
        y_ref[0, h] = (ws[0] * outs[0] + ws[1] * outs[1] + ws[2] * outs[2]) / (ws[0] + ws[1] + ws[2])


def dilated_sample(qkv, bufs, bias_dist):
    bsz = qkv.shape[0]
    hb = SAMPLE_HEADS_PER_STEP
    n_hg = N_HEADS // hb
    parts = qkv.reshape(bsz, 3, 3, N_HEADS, HEAD_DIM)
    q_col = parts[:, :, 0].reshape(bsz, 3, N_HEADS, HEAD_DIM, 1)
    new_col = parts[:, :, 1:3].reshape(bsz, 3, 2, N_HEADS, HEAD_DIM, 1)
    bufs_t, adds = [], []
    for buf, (window, dil) in zip(bufs, DIL_GROUPS):
        assert buf.shape[1] == window
        bufs_t.append(buf.transpose(0, 2, 3, 4, 1))
        pos = np.arange(window)
        add = jnp.take(bias_dist, jnp.asarray(window - pos, jnp.int32), axis=1)
        adds.append(jnp.where(jnp.asarray(pos % dil == 0)[None], add, NEG).reshape(n_hg, hb, window))
    add0 = jnp.broadcast_to(bias_dist[:, :1], (N_HEADS, LANES)).reshape(n_hg, hb, LANES)

    def buf_spec(window):
        return pl.BlockSpec((1, 2, hb, HEAD_DIM, window), lambda b, j: (b, 0, j, 0, 0))

    windows = [w for w, _ in DIL_GROUPS]
    res = pl.pallas_call(
        _dil_sample_body,
        grid=(bsz, n_hg),
        in_specs=[pl.BlockSpec((1, 3, hb, HEAD_DIM, 1), lambda b, j: (b, 0, j, 0, 0)),
                  pl.BlockSpec((1, 3, 2, hb, HEAD_DIM, 1), lambda b, j: (b, 0, 0, j, 0, 0))]
                 + [buf_spec(w) for w in windows]
                 + [pl.BlockSpec((1, hb, w), lambda b, j: (j, 0, 0)) for w in windows]
                 + [pl.BlockSpec((1, hb, LANES), lambda b, j: (j, 0, 0))],
        out_specs=[pl.BlockSpec((1, hb, HEAD_DIM, 1), lambda b, j: (b, j, 0, 0))] + [buf_spec(w) for w in windows],
        out_shape=[jax.ShapeDtypeStruct((bsz, N_HEADS, HEAD_DIM, 1), F32)]
                  + [jax.ShapeDtypeStruct(bt.shape, F32) for bt in bufs_t],
        compiler_params=_params("parallel", "parallel"),
        name="dilated_sample",
    )(q_col, new_col, *bufs_t, *adds, add0)
    new_bufs = tuple(o.transpose(0, 4, 1, 2, 3) for o in res[1:])
    return res[0].reshape(bsz, D_MODEL), new_bufs


def _diff_lambda(lp, lam_init):
    a = jnp.sum(lp[0:1] * lp[1:2], axis=-1, keepdims=True)
    b = jnp.sum(lp[2:3] * lp[3:4], axis=-1, keepdims=True)
    return jnp.exp(a) - jnp.exp(b) + lam_init


def _diff_prompt_body(q_ref, k_ref, v_ref, rb_ref, lam_ref, g_ref, o_ref, bias_ref, m_ref, acc_ref,
                      *, t, n_near, lam_init):
    qi = pl.program_id(2)

    @pl.when(qi == 0)
    def _():
        for slot in range(2):
            for c in range(n_near):
                win = jnp.broadcast_to(rb_ref[0, slot:slot + 1, c * t:(c + 2) * t], (t, 2 * t))
                bias_ref[slot, c] = pltpu.roll(win, t + 1, 1, stride=1, stride_axis=0)[:, :t]

    q = q_ref[...]
    lo, hi = _half_ones(q.shape)
    qs = (q * lo, q * hi)
    m_ref[...] = jnp.full(m_ref.shape, NEG, F32)
    acc_ref[...] = jnp.zeros(acc_ref.shape, F32)
    ones = jnp.ones((t, LANES), BF16)

    def chunk(kj, c):
        start = pl.multiple_of(kj * t, t)
        k = k_ref[pl.ds(start, t), :]
        rhs = jnp.concatenate([v_ref[pl.ds(start, t), :], ones], axis=1)
        for slot in range(2):
            s = _dot_nt(qs[slot], k)
            if c is not None:
                s = s + bias_ref[slot, c]
            m_prev = m_ref[slot]
            m_new = jnp.maximum(m_prev, jnp.max(s, axis=-1, keepdims=True))
            alpha = jnp.exp(m_prev - m_new)
            p = jnp.exp(s - jnp.tile(m_new, (1, t // LANES)))
            acc_ref[slot] = jnp.tile(alpha, (1, 2)) * acc_ref[slot] + _dot(p.astype(BF16), rhs)
            m_ref[slot] = m_new

    def far_body(kj, carry):
        chunk(kj, None)
        return carry

    lax.fori_loop(0, jnp.maximum(qi - (n_near - 1), 0), far_body, 0)
    for c in range(n_near):
        kj = qi - (n_near - 1) + c

        @pl.when(kj >= 0)
        def _(kj=kj, c=c):
            chunk(kj, c)

    lam = _diff_lambda(lam_ref[...], lam_init)
    o = (acc_ref[0, :, :LANES] / acc_ref[0, :, LANES:]
         - lam * (acc_ref[1, :, :LANES] / acc_ref[1, :, LANES:]))
    o = o * lax.rsqrt(jnp.mean(o * o, axis=-1, keepdims=True) + RMS_EPS)
    o_ref[...] = (o * g_ref[...] * (1.0 - lam_init)).astype(o_ref.dtype)


def diff_prompt(q, kv, rev_bias, lam_params, subln_g, lam_init, bsz, seq, t=512):
    nq = seq // t
    n_near = rev_bias.shape[-1] // t - 1
    body = functools.partial(_diff_prompt_body, t=t, n_near=n_near, lam_init=lam_init)
    return pl.pallas_call(
        body,
        grid=(bsz, H_DIFF, nq),
        in_specs=[pl.BlockSpec((t, LANES), lambda b, h, i: (b * nq + i, h)),
                  pl.BlockSpec((seq, LANES), lambda b, h, i: (b, h)),
                  pl.BlockSpec((seq, LANES), lambda b, h, i: (b, H_DIFF + h)),
                  pl.BlockSpec((1, 2, (n_near + 1) * t), lambda b, h, i: (h, 0, 0)),
                  pl.BlockSpec((4, HEAD_DIM), lambda b, h, i: (0, 0)),
                  pl.BlockSpec((1, LANES), lambda b, h, i: (0, 0))],
        out_specs=pl.BlockSpec((t, LANES), lambda b, h, i: (b * nq + i, h)),
        out_shape=jax.ShapeDtypeStruct((bsz * seq, D_MODEL), BF16),
        scratch_shapes=[pltpu.VMEM((2, n_near, t, t), F32), pltpu.VMEM((2, t, LANES), F32),
                        pltpu.VMEM((2, t, 2 * LANES), F32)],
        compiler_params=_params("parallel", "parallel", "arbitrary"),
        name="diff_prompt",
    )(q, kv, kv, rev_bias, lam_params, subln_g.reshape(1, LANES))


def _diff_rev_bias(bias_dist, t):
    n_near = -(-(FAR_DIST + t - 1) // t)
    d_max = n_near * t - 1
    assert bias_dist.shape[1] > d_max
    rel = bias_dist[:, :d_max + 1] - bias_dist[:, -1:]
    rev = jnp.concatenate([rel[:, ::-1], jnp.full((N_HEADS, t), NEG, F32)], axis=1)
    return rev.reshape(H_DIFF, 2, (n_near + 1) * t)


PAGES_PER_STEP = 8


def _paged_sample_body(pt_ref, qkv_ref, *refs, kind, lam_init):
    page_refs = refs[:PAGES_PER_STEP]
    add_ref, aux_ref, g_ref, o_ref, m_ref, l_ref, acc_ref = refs[PAGES_PER_STEP:]
    c = pl.program_id(1)
    rows = 2 * H_DIFF if kind == "diff" else N_HEADS

    if kind == "diff":
        r = lax.broadcasted_iota(jnp.int32, (rows, D_MODEL), 0)
        col = lax.broadcasted_iota(jnp.int32, (rows, D_MODEL), 1)
        start = jnp.where(r < H_DIFF, r * LANES, (r - H_DIFF) * LANES + HEAD_DIM)
        qmask = (col >= start) & (col < start + HEAD_DIM)
    else:
        qmask = _block_diag_mask(N_HEADS, HEAD_DIM)
    q_bd = _block_diag_rows(qkv_ref[0, :, :D_MODEL] * ATTN_SCALE, qmask)

    @pl.when(c == 0)
    def _():
        m_ref[...] = jnp.full(m_ref.shape, NEG, F32)
        l_ref[...] = jnp.zeros(l_ref.shape, F32)
        acc_ref[...] = jnp.zeros(acc_ref.shape, F32)

    s = jnp.concatenate([_dot_nt(q_bd, pr[0, :, :D_MODEL].astype(BF16)) for pr in page_refs], axis=-1)
    s = s + add_ref[0]
    m_prev = m_ref[...]
    m_new = jnp.maximum(m_prev, jnp.max(s, axis=-1, keepdims=True))
    alpha = jnp.exp(m_prev - m_new)
    p = jnp.exp(s - m_new[:, :1]).astype(BF16)
    l_ref[...] = alpha * l_ref[...] + jnp.sum(p.astype(F32), axis=-1, keepdims=True)
    pv = _dot(p[:, :PAGE_SIZE], page_refs[0][0, :, D_MODEL:].astype(BF16))
    for i in range(1, PAGES_PER_STEP):
        pv = pv + _dot(p[:, i * PAGE_SIZE:(i + 1) * PAGE_SIZE], page_refs[i][0, :, D_MODEL:].astype(BF16))
    acc_ref[...] = alpha[:, :1] * acc_ref[...] + pv
    m_ref[...] = m_new

    @pl.when(c == pl.num_programs(1) - 1)
    def _():
        k_new = qkv_ref[0, :, D_MODEL:2 * D_MODEL].astype(BF16).astype(F32)
        v_new = qkv_ref[0, :, 2 * D_MODEL:].astype(BF16).astype(F32)
        s_new = jnp.sum(q_bd.astype(F32) * k_new, axis=-1, keepdims=True) + aux_ref[4:4 + rows, :1]
        m_prev = m_ref[...][:, :1]
        m_fin = jnp.maximum(m_prev, s_new)
        alpha = jnp.exp(m_prev - m_fin)
        p_new = jnp.exp(s_new - m_fin)
        den = alpha * l_ref[...][:, :1] + p_new
        res = (alpha * acc_ref[...] + p_new.astype(BF16).astype(F32) * v_new) / den
        if kind == "diff":
            lam = _diff_lambda(aux_ref[0:4, :HEAD_DIM], lam_init)
            o = res[:H_DIFF] - lam * res[H_DIFF:]
            r8 = lax.broadcasted_iota(jnp.int32, o.shape, 0)
            c8 = lax.broadcasted_iota(jnp.int32, o.shape, 1)
            o = jnp.where((c8 >= r8 * LANES) & (c8 < (r8 + 1) * LANES), o, 0.0)
            ms = jnp.sum(o * o, axis=-1, keepdims=True) / LANES
            o = o * lax.rsqrt(ms + RMS_EPS)
            o_ref[0] = jnp.sum(o, axis=0, keepdims=True) * g_ref[...] * (1.0 - lam_init)
        else:
            o_ref[0] = jnp.sum(jnp.where(qmask, res, 0.0), axis=0, keepdims=True)


def paged_sample(qkv, cache, page_table, add_past, aux, subln_g, kind, lam_init=0.0):
    bsz = qkv.shape[0]
    n_pages = page_table.shape[1]
    assert n_pages % PAGES_PER_STEP == 0
    n_steps = n_pages // PAGES_PER_STEP
    rows = add_past.shape[1]
    pages = cache.reshape(cache.shape[0], PAGE_SIZE, 2 * D_MODEL)
    per_seq = add_past.shape[0] == bsz

    def page_spec(i):
        return pl.BlockSpec((1, PAGE_SIZE, 2 * D_MODEL),
                            lambda b, c, pt: (pt[b * n_pages + c * PAGES_PER_STEP + i], 0, 0))

    grid_spec = pltpu.PrefetchScalarGridSpec(
        num_scalar_prefetch=1,
        grid=(bsz, n_steps),
        in_specs=[pl.BlockSpec((1, 1, 3 * D_MODEL), lambda b, c, pt: (b, 0, 0))]
                 + [page_spec(i) for i in range(PAGES_PER_STEP)]
                 + [pl.BlockSpec((1, rows, PAGES_PER_STEP * PAGE_SIZE),
                                 (lambda b, c, pt: (b, 0, c)) if per_seq else (lambda b, c, pt: (0, 0, c))),
                    pl.BlockSpec(aux.shape, lambda b, c, pt: (0, 0)),
                    pl.BlockSpec((1, D_MODEL), lambda b, c, pt: (0, 0))],
        out_specs=pl.BlockSpec((1, 1, D_MODEL), lambda b, c, pt: (b, 0, 0)),
        scratch_shapes=[pltpu.VMEM((rows, LANES), F32), pltpu.VMEM((rows, LANES), F32),
                        pltpu.VMEM((rows, D_MODEL), F32)],
    )
    out = pl.pallas_call(
        functools.partial(_paged_sample_body, kind=kind, lam_init=lam_init),
        grid_spec=grid_spec,
        out_shape=jax.ShapeDtypeStruct((bsz, 1, D_MODEL), F32),
        compiler_params=_params("parallel", "arbitrary"),
        name="paged_sample_" + kind,
    )(page_table.reshape(-1), qkv.reshape(bsz, 1, 3 * D_MODEL), *([pages] * PAGES_PER_STEP),
      add_past, aux, subln_g)
    return out.reshape(bsz, D_MODEL)


def _split3(x):
    hi = x.astype(BF16)
    r1 = x - hi.astype(F32)
    mid = r1.astype(BF16)
    lo = (r1 - mid.astype(F32)).astype(BF16)
    return hi, mid, lo


def _tri_cumsum(x, tri):
    hi, mid, lo = _split3(x)
    return _dot(tri, hi) + _dot(tri, mid) + _dot(tri, lo)


def _lower_tri(n):
    r = lax.broadcasted_iota(jnp.int32, (n, n), 0)
    c = lax.broadcasted_iota(jnp.int32, (n, n), 1)
    return jnp.where(r >= c, 1.0, 0.0).astype(BF16)


def _cumsum_body(x_ref, o_ref, carry_ref, *, chunk):
    tri = _lower_tri(chunk)
    carry_ref[...] = jnp.zeros(carry_ref.shape, F32)

    def body(i, c):
        start = pl.multiple_of(i * chunk, chunk)
        cum = _tri_cumsum(x_ref[0, pl.ds(start, chunk), :], tri) + carry_ref[...]
        o_ref[0, pl.ds(start, chunk), :] = cum
        carry_ref[...] = cum[chunk - 1:chunk, :]
        return c

    lax.fori_loop(0, x_ref.shape[1] // chunk, body, 0)


def seq_cumsum(x, chunk=128):
    bsz, seq, c = x.shape
    assert seq % chunk == 0
    return pl.pallas_call(
        functools.partial(_cumsum_body, chunk=chunk),
        grid=(bsz,),
        in_specs=[pl.BlockSpec((1, seq, c), lambda b: (b, 0, 0))],
        out_specs=pl.BlockSpec((1, seq, c), lambda b: (b, 0, 0)),
        out_shape=jax.ShapeDtypeStruct((bsz, seq, c), F32),
        scratch_shapes=[pltpu.VMEM((1, c), F32)],
        compiler_params=_params("parallel"),
        name="seq_cumsum",
    )(x)


def _fox_prompt_body(q_ref, k_ref, v_ref, ccol_ref, crow_ref, o_ref, m_ref, acc_ref, *, t):
    qi = pl.program_id(2)
    q = q_ref[...]
    ones = _half_ones(q.shape)
    qs = [q * one for one in ones]
    m_ref[...] = jnp.full(m_ref.shape, NEG, F32)
    acc_ref[...] = jnp.zeros(acc_ref.shape, F32)
    cq = ccol_ref[0, 0]
    cref = [cq[0:1, s:s + 1] for s in range(2)]
    a_rep = [jnp.broadcast_to(cq[:, s:s + 1] - cref[s], (t, LANES)) for s in range(2)]

    def chunk(kj, diag):
        start = pl.multiple_of(kj * t, t)
        k = k_ref[pl.ds(start, t), :]
        v = v_ref[pl.ds(start, t), :]
        crow = crow_ref[0, 0, :, pl.ds(start, t)]
        for slot in range(2):
            s = (_dot_nt(qs[slot], k) + jnp.tile(a_rep[slot], (1, t // LANES))
                 - (crow[slot:slot + 1, :] - cref[slot]))
            if diag:
                r = lax.broadcasted_iota(jnp.int32, s.shape, 0)
                c = lax.broadcasted_iota(jnp.int32, s.shape, 1)
                s = jnp.where(r >= c, s, NEG)
            m_prev = m_ref[slot]
            m_new = jnp.maximum(m_prev, jnp.max(s, axis=-1, keepdims=True))
            alpha = jnp.exp(m_prev - m_new)
            p = jnp.exp(s - jnp.tile(m_new, (1, t // LANES)))
            rhs = v * ones[slot] + ones[1 - slot]
            acc_ref[slot] = alpha * acc_ref[slot] + _dot(p.astype(BF16), rhs)
            m_ref[slot] = m_new

    def body(kj, carry):
        chunk(kj, False)
        return carry

    lax.fori_loop(0, qi, body, 0)
    chunk(qi, True)
    lo, _ = _half_masks((t, LANES))
    num = jnp.where(lo, acc_ref[0], acc_ref[1])
    den = jnp.where(lo, pltpu.roll(acc_ref[0], HEAD_DIM, 1), pltpu.roll(acc_ref[1], HEAD_DIM, 1))
    o_ref[...] = (num / den).astype(o_ref.dtype)


def fox_prompt(q, kv, cum, bsz, seq, t=512):
    nq = seq // t
    n_pair = N_HEADS // 2
    ccol = cum.reshape(bsz, seq, n_pair, 2).transpose(0, 2, 1, 3)
    crow = cum.reshape(bsz, seq, n_pair, 2).transpose(0, 2, 3, 1)
    return pl.pallas_call(
        functools.partial(_fox_prompt_body, t=t),
        grid=(bsz, n_pair, nq),
        in_specs=[pl.BlockSpec((t, LANES), lambda b, h, i: (b * nq + i, h)),
                  pl.BlockSpec((seq, LANES), lambda b, h, i: (b, h)),
                  pl.BlockSpec((seq, LANES), lambda b, h, i: (b, n_pair + h)),
                  pl.BlockSpec((1, 1, t, 2), lambda b, h, i: (b, h, i, 0)),
                  pl.BlockSpec((1, 1, 2, seq), lambda b, h, i: (b, h, 0, 0))],
        out_specs=pl.BlockSpec((t, LANES), lambda b, h, i: (b * nq + i, h)),
        out_shape=jax.ShapeDtypeStruct((bsz * seq, D_MODEL), BF16),
        scratch_shapes=[pltpu.VMEM((2, t, LANES), F32), pltpu.VMEM((2, t, LANES), F32)],
        compiler_params=_params("parallel", "parallel", "arbitrary"),
        name="fox_prompt",
    )(q, kv, kv, ccol, crow)


def _fox_sample_body(pt_ref, q_ref, new_ref, lfn_ref, *refs, n_pages):
    kv_refs = refs[:PAGES_PER_STEP]
    lf_refs = refs[PAGES_PER_STEP:PAGES_PER_STEP + n_pages]
    y_ref, dec_ref, s_ref, m_ref, l_ref, acc_ref = refs[PAGES_PER_STEP + n_pages:]
    c = pl.program_id(1)
    width = PAGES_PER_STEP * PAGE_SIZE

    @pl.when(c == 0)
    def _():
        r = lax.broadcasted_iota(jnp.int32, (PAGE_SIZE, PAGE_SIZE), 0)
        col = lax.broadcasted_iota(jnp.int32, (PAGE_SIZE, PAGE_SIZE), 1)
        upper = jnp.where(r > col, 1.0, 0.0).astype(BF16)
        carry = lfn_ref[0]
        for i in reversed(range(n_pages)):
            x = lf_refs[i][0]
            hi, mid, lo = _split3(x)
            dec_ref[:, i * PAGE_SIZE:(i + 1) * PAGE_SIZE] = _dot(hi, upper) + _dot(mid, upper) + _dot(lo, upper) + carry
            carry = carry + jnp.sum(x, axis=1, keepdims=True)
        m_ref[...] = jnp.full(m_ref.shape, NEG, F32)
        l_ref[...] = jnp.zeros(l_ref.shape, F32)
        acc_ref[...] = jnp.zeros(acc_ref.shape, F32)

    qs = [_bf16_round(q_ref[0, h] * ATTN_SCALE) for h in range(N_HEADS)]
    for i in range(PAGES_PER_STEP):
        for h in range(N_HEADS):
            s_ref[h:h + 1, i * PAGE_SIZE:(i + 1) * PAGE_SIZE] = jnp.sum(
                _bf16_round(kv_refs[i][0, 0, h]) * qs[h], axis=0, keepdims=True)
    s = s_ref[...] + dec_ref[:, pl.ds(pl.multiple_of(c * width, width), width)]
    m_prev = m_ref[...]
    m_new = jnp.maximum(m_prev, jnp.max(s, axis=1, keepdims=True))
    alpha = jnp.exp(m_prev - m_new)
    p = jnp.exp(s - m_new[:, :1])
    l_ref[...] = alpha * l_ref[...] + jnp.sum(p, axis=1, keepdims=True)
    m_ref[...] = m_new
    s_ref[...] = _bf16_round(p)
    for h in range(N_HEADS):
        acc = alpha[h:h + 1, :] * acc_ref[h]
        for i in range(PAGES_PER_STEP):
            acc = acc + _bf16_round(kv_refs[i][0, 1, h]) * s_ref[h:h + 1, i * PAGE_SIZE:(i + 1) * PAGE_SIZE]
        acc_ref[h] = acc

    @pl.when(c == pl.num_programs(1) - 1)
    def _():
        for h in range(N_HEADS):
            k_new, v_new = _bf16_round(new_ref[0, 0, h]), _bf16_round(new_ref[0, 1, h])
            s_new = jnp.sum(qs[h] * k_new, axis=0, keepdims=True)
            m_h, l_h = m_ref[h:h + 1, :1], l_ref[h:h + 1, :1]
            m_fin = jnp.maximum(m_h, s_new)
            a_fin = jnp.exp(m_h - m_fin)
            p_new = jnp.exp(s_new - m_fin)
            num = a_fin * jnp.sum(acc_ref[h], axis=1, keepdims=True) + _bf16_round(p_new) * v_new
            y_ref[0, h] = num / (a_fin * l_h + p_new)


def fox_sample(qkv, cache_kv, cache_logf, page_table, logf_new):
    bsz, n_pages = page_table.shape
    assert n_pages % PAGES_PER_STEP == 0
    kv_t = cache_kv.transpose(0, 2, 3, 4, 1)
    lf_t = cache_logf.transpose(0, 2, 1)
    q_col = qkv[:, :D_MODEL].reshape(bsz, N_HEADS, HEAD_DIM, 1)
    new_col = qkv[:, D_MODEL:].reshape(bsz, 2, N_HEADS, HEAD_DIM, 1)

    def kv_spec(i):
        return pl.BlockSpec((1, 2, N_HEADS, HEAD_DIM, PAGE_SIZE),
                            lambda b, c, pt: (pt[b * n_pages + c * PAGES_PER_STEP + i], 0, 0, 0, 0))

    def lf_spec(i):
        return pl.BlockSpec((1, N_HEADS, PAGE_SIZE), lambda b, c, pt: (pt[b * n_pages + i], 0, 0))

    past = n_pages * PAGE_SIZE
    grid_spec = pltpu.PrefetchScalarGridSpec(
        num_scalar_prefetch=1,
        grid=(bsz, n_pages // PAGES_PER_STEP),
        in_specs=[pl.BlockSpec((1, N_HEADS, HEAD_DIM, 1), lambda b, c, pt: (b, 0, 0, 0)),
                  pl.BlockSpec((1, 2, N_HEADS, HEAD_DIM, 1), lambda b, c, pt: (b, 0, 0, 0, 0)),
                  pl.BlockSpec((1, N_HEADS, 1), lambda b, c, pt: (b, 0, 0))]
                 + [kv_spec(i) for i in range(PAGES_PER_STEP)] + [lf_spec(i) for i in range(n_pages)],
        out_specs=pl.BlockSpec((1, N_HEADS, HEAD_DIM, 1), lambda b, c, pt: (b, 0, 0, 0)),
        scratch_shapes=[pltpu.VMEM((N_HEADS, past), F32),
                        pltpu.VMEM((N_HEADS, PAGES_PER_STEP * PAGE_SIZE), F32),
                        pltpu.VMEM((N_HEADS, LANES), F32), pltpu.VMEM((N_HEADS, LANES), F32),
                        pltpu.VMEM((N_HEADS, HEAD_DIM, PAGE_SIZE), F32)],
    )
    out = pl.pallas_call(
        functools.partial(_fox_sample_body, n_pages=n_pages),
        grid_spec=grid_spec,
        out_shape=jax.ShapeDtypeStruct((bsz, N_HEADS, HEAD_DIM, 1), F32),
        compiler_params=_params("parallel", "arbitrary"),
        name="fox_sample",
    )(page_table.reshape(-1), q_col, new_col, logf_new.reshape(bsz, N_HEADS, 1),
      *([kv_t] * PAGES_PER_STEP), *([lf_t] * n_pages))
    return out.reshape(bsz, D_MODEL)


def _pad_cols(w, n):
    return jnp.pad(w, [(0, 0)] * (w.ndim - 1) + [(0, n - w.shape[-1])])


def _split_gu(w_gu):
    gate, up = w_gu[..., :D_FF], w_gu[..., D_FF:]
    return _pad_cols(gate, D_FF_PAD).astype(BF16), _pad_cols(up, D_FF_PAD).astype(BF16)


def _pad_down(w_down):
    pad = [(0, 0)] * (w_down.ndim - 2) + [(0, D_FF_PAD - D_FF), (0, 0)]
    return jnp.pad(w_down, pad).astype(BF16)


def _mixer_a(hp, hs, bufs, w_qkv, w_o, ln_g, ln_b, bias_dist, bsz, seq):
    w = w_qkv.astype(BF16)
    col = np.arange(w.shape[1]) % (3 * D_MODEL) < D_MODEL
    wq_scaled = jnp.where(jnp.asarray(col)[None, :], w * ATTN_SCALE, w)
    (qkv_s,) = linear(hs, w, [F32])

    tail = min(DIL_GROUPS[-1][0], seq)
    kv_cols = np.concatenate([np.arange(g * 3 * D_MODEL + D_MODEL, (g + 1) * 3 * D_MODEL) for g in range(3)])
    x_tail = hp.reshape(bsz, seq, D_MODEL)[:, seq - tail:].reshape(bsz * tail, D_MODEL)
    (kv_tail,) = linear(x_tail, w[:, kv_cols], [F32])
    kv_tail = kv_tail.reshape(bsz, tail, 3, 2, N_HEADS, HEAD_DIM)
    bufs_p = tuple(kv_tail[:, tail - min(wd, seq):, g] for g, (wd, _) in enumerate(DIL_GROUPS))

    tiles = _dil_bias_tiles(bias_dist)
    outs, lses = [], []
    for g, (_, dil) in enumerate(DIL_GROUPS):
        qkv_g = linear_dil(hp, wq_scaled[:, g * 3 * D_MODEL:(g + 1) * 3 * D_MODEL], dil, bsz, seq)
        o, l = dilated_prompt_group(qkv_g, tiles[g], dil, bsz, seq)
        outs.append(o)
        lses.append(l)
    hp = merge_o_ln(outs, lses, w_o, hp, ln_g, ln_b, bsz, seq)

    ys, bufs_s = dilated_sample(qkv_s, bufs, bias_dist)
    hs = linear_ln(ys, w_o, hs, ln_g, ln_b)
    return hp, hs, bufs_p, bufs_s


def _mixer_b(hp, hs, cache_kv, page_table, w_qkv, lam_params, subln_g, w_o, ln_g, ln_b, bias_dist,
             layer_idx, bsz, seq):
    w = w_qkv.astype(BF16)
    lam_init = 0.8 - 0.6 * math.exp(-0.3 * layer_idx)
    (q_p,) = linear(hp, w[:, :D_MODEL], [BF16], scale=ATTN_SCALE)
    kv_p, kv_pb = linear(hp, w[:, D_MODEL:], [F32, BF16])
    (qkv_s,) = linear(hs, w, [F32])
    t = min(512, seq)
    o = diff_prompt(q_p, kv_pb, _diff_rev_bias(bias_dist, t), lam_params, subln_g, lam_init, bsz, seq, t)
    hp = linear_ln(o, w_o, hp, ln_g, ln_b)

    dec = hs.shape[0]
    n_past = page_table.shape[1] * PAGE_SIZE
    order = np.array([2 * (r % H_DIFF) + r // H_DIFF for r in range(2 * H_DIFF)])
    bias_rows = bias_dist[order]
    dist = n_past - np.arange(n_past)
    add_past = jnp.take(bias_rows, jnp.asarray(dist, jnp.int32), axis=1)[None]
    aux = jnp.zeros((4 + 2 * H_DIFF + 4, LANES), F32)
    aux = aux.at[0:4, :HEAD_DIM].set(lam_params.astype(F32))
    aux = aux.at[4:4 + 2 * H_DIFF, :].set(jnp.broadcast_to(bias_rows[:, :1], (2 * H_DIFF, LANES)))
    ys = paged_sample(qkv_s, cache_kv, page_table, add_past, aux,
                      jnp.tile(subln_g.astype(F32), H_DIFF).reshape(1, D_MODEL), "diff", lam_init)
    hs = linear_ln(ys, w_o, hs, ln_g, ln_b)
    st_p = kv_p.reshape(bsz, seq, 2, H_DIFF, 2 * HEAD_DIM)
    st_s = qkv_s[:, D_MODEL:].reshape(dec, 1, 2, H_DIFF, 2 * HEAD_DIM)
    return hp, hs, st_p, st_s


def _mixer_c(hp, hs, cache_kv, cache_logf, page_table, w_qkv, w_f, b_f, w_o, ln_g, ln_b, bsz, seq):
    w = w_qkv.astype(BF16)
    wf = _pad_cols(w_f, LANES).astype(BF16)
    bf = _pad_cols(b_f.reshape(1, N_HEADS), LANES).astype(F32)
    (q_p,) = linear(hp, w[:, :D_MODEL], [BF16], scale=ATTN_SCALE)
    kv_p, kv_pb = linear(hp, w[:, D_MODEL:], [F32, BF16])
    (qkv_s,) = linear(hs, w, [F32])
    logf_p = forget_log(hp, wf, bf)[:, :N_HEADS].reshape(bsz, seq, N_HEADS)
    logf_s = forget_log(hs, wf, bf)[:, :N_HEADS]
    cum = seq_cumsum(logf_p)
    o = fox_prompt(q_p, kv_pb, cum, bsz, seq)
    hp = linear_ln(o, w_o, hp, ln_g, ln_b)

    dec = hs.shape[0]
    ys = fox_sample(qkv_s, cache_kv, cache_logf, page_table, logf_s)
    hs = linear_ln(ys, w_o, hs, ln_g, ln_b)
    st_p = (kv_p.reshape(bsz, seq, 2, N_HEADS, HEAD_DIM), logf_p)
    st_s = (qkv_s[:, D_MODEL:].reshape(dec, 1, 2, N_HEADS, HEAD_DIM), logf_s.reshape(dec, 1, N_HEADS))
    return hp, hs, st_p, st_s


def _dense_ffn(h, w_gu, w_down, ln_g, ln_b):
    gate, up = _split_gu(w_gu)
    return ffn_ln(h, gate[None], up[None], _pad_down(w_down)[None], None, ln_g, ln_b)


def _moe_ffn(h, w_router, w_gu, w_down, ln_g, ln_b):
    combine = moe_router(h, _pad_cols(w_router, LANES).astype(BF16))
    gate, up = _split_gu(w_gu)
    return ffn_ln(h, gate, up, _pad_down(w_down), combine, ln_g, ln_b)


def kernel(x_prompt, x_sample, cache_l0_kv_w128, cache_l0_kv_w512, cache_l0_kv_w2048, cache_l1_kv, cache_l2_kv, cache_l2_logf, cache_l3_kv_w128, cache_l3_kv_w512, cache_l3_kv_w2048, page_table, p_prompt, p_sample, rel_bias, ln_g, ln_b, ple_w_in, ple_w_gate, ple_b_gate, a_w_qkv, a_w_o, b_w_qkv, b_lambda, b_subln_g, b_w_o, c_w_qkv, c_w_f, c_b_f, c_w_o, ffn_w_gu, ffn_w_down, moe_router, moe_w_gu, moe_w_down):
    bsz, seq, d = x_prompt.shape
    dec = x_sample.shape[0]
    assert d == D_MODEL and x_sample.shape[1] == 1 and seq % DIL_GROUPS[-1][0] == 0
    caches = ((cache_l0_kv_w128, cache_l0_kv_w512, cache_l0_kv_w2048), (cache_l1_kv,),
              (cache_l2_kv, cache_l2_logf), (cache_l3_kv_w128, cache_l3_kv_w512, cache_l3_kv_w2048))
    bias_dist = _bias_by_distance(rel_bias, max(MAX_DISTANCE, page_table.shape[1] * PAGE_SIZE))
    hp = x_prompt.reshape(bsz * seq, d)
    hs = x_sample.reshape(dec, d)
    states = []
    for i in range(DEPTH):
        kind, j = i % 3, i // 3
        g0, b0, g1, b1 = ln_g[i, 0], ln_b[i, 0], ln_g[i, 1], ln_b[i, 1]
        if kind == 0:
            hp, hs, st_p, st_s = _mixer_a(hp, hs, caches[i], a_w_qkv[j], a_w_o[j].astype(BF16), g0, b0,
                                          bias_dist, bsz, seq)
        elif kind == 1:
            hp, hs, st_p, st_s = _mixer_b(hp, hs, caches[i][0], page_table, b_w_qkv[j], b_lambda[j],
                                          b_subln_g[j], b_w_o[j].astype(BF16), g0, b0, bias_dist, i, bsz, seq)
            st_p, st_s = (st_p,), (st_s,)
        else:
            hp, hs, st_p, st_s = _mixer_c(hp, hs, caches[i][0], caches[i][1], page_table, c_w_qkv[j],
                                          c_w_f[j], c_b_f[j], c_w_o[j].astype(BF16), g0, b0, bsz, seq)
        states.append((st_p, st_s))
        f = i // 2
        if i % 2 == 0:
            hp = _dense_ffn(hp, ffn_w_gu[f], ffn_w_down[f], g1, b1)
            hs = _dense_ffn(hs, ffn_w_gu[f], ffn_w_down[f], g1, b1)
        else:
            hp = _moe_ffn(hp, moe_router[f], moe_w_gu[f], moe_w_down[f], g1, b1)
            hs = _moe_ffn(hs, moe_router[f], moe_w_gu[f], moe_w_down[f], g1, b1)
        wg, wi = ple_w_gate[i].astype(BF16), ple_w_in[i].astype(BF16)
        hp = per_layer_embed(hp, p_prompt[i].reshape(bsz * seq, D_PLE), wg, ple_b_gate[i], wi)
        hs = per_layer_embed(hs, p_sample[i].reshape(dec, D_PLE), wg, ple_b_gate[i], wi)
    (l0_p, l0_s), (l1_p, l1_s), (l2_p, l2_s), (l3_p, l3_s) = states
    return (hp.reshape(bsz, seq, d), hs.reshape(dec, 1, d),
            l0_p[0], l0_s[0], l0_p[1], l0_s[1], l0_p[2], l0_s[2],
            l1_p[0], l1_s[0],
            l2_p[0], l2_s[0], l2_p[1], l2_s[1],
            l3_p[0], l3_s[0], l3_p[1], l3_s[1], l3_p[2], l3_s[2])
```

```python
import functools
import math

import numpy as np
import jax
import jax.numpy as jnp
from jax import lax
from jax.experimental import pallas as pl
from jax.experimental.pallas import tpu as pltpu

F32 = jnp.float32
BF16 = jnp.bfloat16

D_MODEL = 1024
HEAD_DIM = 64
N_HEADS = 16
H_DIFF = 8
DIL_GROUPS = ((128, 1), (512, 4), (2048, 16))
N_KEYS = 128
N_BUCKETS = 32
MAX_DISTANCE = 2048
D_FF = 2752
D_FF_PAD = 2816
N_EXPERTS = 8
D_PLE = 256
DEPTH = 4
PAGE_SIZE = 128
LN_EPS = 1e-5
RMS_EPS = 1e-5
ALPHA = (2 * DEPTH) ** 0.25
ATTN_SCALE = HEAD_DIM ** -0.5
NEG = -1e30
LANES = 128
FAR_DIST = 1520
VMEM_LIMIT = 56 * 1024 * 1024


def _params(*sem):
    return pltpu.CompilerParams(dimension_semantics=sem, vmem_limit_bytes=VMEM_LIMIT)


def _dot(a, b):
    return jnp.dot(a, b, preferred_element_type=F32)


def _dot_nt(a, b):
    return lax.dot_general(a, b, (((1,), (1,)), ((), ())), preferred_element_type=F32)


def _layer_norm(y, g, b):
    mu = jnp.mean(y, axis=-1, keepdims=True)
    yc = y - mu
    var = jnp.mean(yc * yc, axis=-1, keepdims=True)
    return yc * lax.rsqrt(var + LN_EPS) * g + b


def _sigmoid(x):
    return 1.0 / (1.0 + jnp.exp(-x))


def _linear_body(x_ref, w_ref, *o_refs, scale):
    acc = _dot(x_ref[...].astype(BF16), w_ref[...])
    if scale is not None:
        acc = acc * scale
    for o_ref in o_refs:
        o_ref[...] = acc.astype(o_ref.dtype)


def linear(x, w, out_dtypes, scale=None, tm=512, tn=1024):
    m, k = x.shape
    n = w.shape[1]
    tm, tn = min(tm, m), min(tn, n)
    assert m % tm == 0 and n % tn == 0
    outs = pl.pallas_call(
        functools.partial(_linear_body, scale=scale),
        grid=(m // tm, n // tn),
        in_specs=[pl.BlockSpec((tm, k), lambda i, j: (i, 0)),
                  pl.BlockSpec((k, tn), lambda i, j: (0, j))],
        out_specs=[pl.BlockSpec((tm, tn), lambda i, j: (i, j)) for _ in out_dtypes],
        out_shape=[jax.ShapeDtypeStruct((m, n), d) for d in out_dtypes],
        compiler_params=_params("parallel", "parallel"),
        name="linear",
    )(x, w)
    return outs


def _forget_body(x_ref, w_ref, b_ref, o_ref):
    z = _dot(x_ref[...].astype(BF16), w_ref[...]) + b_ref[...]
    o_ref[...] = -(jnp.maximum(-z, 0.0) + jnp.log1p(jnp.exp(-jnp.abs(z))))


def forget_log(x, w_pad, b_pad, tm=512):
    m, k = x.shape
    tm = min(tm, m)
    assert m % tm == 0
    return pl.pallas_call(
        _forget_body,
        grid=(m // tm,),
        in_specs=[pl.BlockSpec((tm, k), lambda i: (i, 0)),
                  pl.BlockSpec((k, LANES), lambda i: (0, 0)),
                  pl.BlockSpec((1, LANES), lambda i: (0, 0))],
        out_specs=pl.BlockSpec((tm, LANES), lambda i: (i, 0)),
        out_shape=jax.ShapeDtypeStruct((m, LANES), F32),
        compiler_params=_params("parallel"),
        name="forget_log",
    )(x, w_pad, b_pad)


def _linear_ln_body(a_ref, w_ref, r_ref, g_ref, b_ref, o_ref):
    sub = _dot(a_ref[...].astype(BF16), w_ref[...])
    o_ref[...] = _layer_norm(ALPHA * r_ref[...] + sub, g_ref[...], b_ref[...])


def linear_ln(a, w, resid, g, b, tm=512):
    m, k = a.shape
    n = w.shape[1]
    tm = min(tm, m)
    assert m % tm == 0
    return pl.pallas_call(
        _linear_ln_body,
        grid=(m // tm,),
        in_specs=[pl.BlockSpec((tm, k), lambda i: (i, 0)),
                  pl.BlockSpec((k, n), lambda i: (0, 0)),
                  pl.BlockSpec((tm, n), lambda i: (i, 0)),
                  pl.BlockSpec((1, n), lambda i: (0, 0)),
                  pl.BlockSpec((1, n), lambda i: (0, 0))],
        out_specs=pl.BlockSpec((tm, n), lambda i: (i, 0)),
        out_shape=jax.ShapeDtypeStruct((m, n), F32),
        compiler_params=_params("parallel"),
        name="linear_ln",
    )(a, w, resid, g.reshape(1, n), b.reshape(1, n))


def _ple_body(x_ref, p_ref, wg_ref, bg_ref, wp_ref, o_ref):
    x = x_ref[...]
    gate = _sigmoid(_dot(x.astype(BF16), wg_ref[...]) + bg_ref[...])
    o_ref[...] = x + gate * _dot(p_ref[...].astype(BF16), wp_ref[...])


def per_layer_embed(x, p, w_gate, b_gate, w_in, tm=512):
    m, d = x.shape
    dp = p.shape[1]
    tm = min(tm, m)
    assert m % tm == 0
    return pl.pallas_call(
        _ple_body,
        grid=(m // tm,),
        in_specs=[pl.BlockSpec((tm, d), lambda i: (i, 0)),
                  pl.BlockSpec((tm, dp), lambda i: (i, 0)),
                  pl.BlockSpec((d, d), lambda i: (0, 0)),
                  pl.BlockSpec((1, d), lambda i: (0, 0)),
                  pl.BlockSpec((dp, d), lambda i: (0, 0))],
        out_specs=pl.BlockSpec((tm, d), lambda i: (i, 0)),
        out_shape=jax.ShapeDtypeStruct((m, d), F32),
        compiler_params=_params("parallel"),
        name="per_layer_embed",
    )(x, p, w_gate, b_gate.reshape(1, d), w_in)


def _router_body(x_ref, w_ref, o_ref):
    logits = _dot(x_ref[...].astype(BF16), w_ref[...])
    lane = lax.broadcasted_iota(jnp.int32, logits.shape, 1)
    logits = jnp.where(lane < N_EXPERTS, logits, NEG)
    v1 = jnp.max(logits, axis=-1, keepdims=True)
    i1 = jnp.min(jnp.where(logits == v1, lane, LANES), axis=-1, keepdims=True)
    rest = jnp.where(lane == i1, NEG, logits)
    v2 = jnp.max(rest, axis=-1, keepdims=True)
    i2 = jnp.min(jnp.where(rest == v2, lane, LANES), axis=-1, keepdims=True)
    e2 = jnp.exp(v2 - v1)
    g1 = 1.0 / (1.0 + e2)
    g2 = e2 / (1.0 + e2)
    o_ref[...] = jnp.where(lane == i1, g1, 0.0) + jnp.where(lane == i2, g2, 0.0)


def moe_router(x, w_pad, tm=512):
    m, k = x.shape
    tm = min(tm, m)
    assert m % tm == 0
    return pl.pallas_call(
        _router_body,
        grid=(m // tm,),
        in_specs=[pl.BlockSpec((tm, k), lambda i: (i, 0)),
                  pl.BlockSpec((k, LANES), lambda i: (0, 0))],
        out_specs=pl.BlockSpec((tm, LANES), lambda i: (i, 0)),
        out_shape=jax.ShapeDtypeStruct((m, LANES), F32),
        compiler_params=_params("parallel"),
        name="moe_router",
    )(x, w_pad)


def _ffn_body(x_ref, wg_ref, wu_ref, wd_ref, c_ref, g_ref, b_ref, o_ref, xb_ref, acc_ref, *, gated):
    e, f = pl.program_id(1), pl.program_id(2)

    @pl.when((e == 0) & (f == 0))
    def _():
        xb_ref[...] = x_ref[...].astype(BF16)
        acc_ref[...] = jnp.zeros_like(acc_ref)

    xb = xb_ref[...]
    gate = _dot(xb, wg_ref[0])
    up = _dot(xb, wu_ref[0])
    h = (gate * _sigmoid(gate) * up).astype(BF16)
    y = _dot(h, wd_ref[0])
    if gated:
        c = c_ref[...]
        lane = lax.broadcasted_iota(jnp.int32, c.shape, 1)
        y = y * jnp.sum(jnp.where(lane == e, c, 0.0), axis=-1, keepdims=True)
    acc_ref[...] += y

    @pl.when((e == pl.num_programs(1) - 1) & (f == pl.num_programs(2) - 1))
    def _():
        o_ref[...] = _layer_norm(ALPHA * x_ref[...] + acc_ref[...], g_ref[...], b_ref[...])


def ffn_ln(x, w_gate, w_up, w_down, combine, g, b, tm=1024, tf=256):
    m, d = x.shape
    n_e, _, f_pad = w_gate.shape
    tm = min(tm, m)
    assert m % tm == 0 and f_pad % tf == 0
    gated = combine is not None
    if not gated:
        combine = jnp.zeros((m, LANES), F32)
    return pl.pallas_call(
        functools.partial(_ffn_body, gated=gated),
        grid=(m // tm, n_e, f_pad // tf),
        in_specs=[pl.BlockSpec((tm, d), lambda i, e, f: (i, 0)),
                  pl.BlockSpec((1, d, tf), lambda i, e, f: (e, 0, f)),
                  pl.BlockSpec((1, d, tf), lambda i, e, f: (e, 0, f)),
                  pl.BlockSpec((1, tf, d), lambda i, e, f: (e, f, 0)),
                  pl.BlockSpec((tm, LANES), lambda i, e, f: (i, 0)),
                  pl.BlockSpec((1, d), lambda i, e, f: (0, 0)),
                  pl.BlockSpec((1, d), lambda i, e, f: (0, 0))],
        out_specs=pl.BlockSpec((tm, d), lambda i, e, f: (i, 0)),
        out_shape=jax.ShapeDtypeStruct((m, d), F32),
        scratch_shapes=[pltpu.VMEM((tm, d), BF16), pltpu.VMEM((tm, d), F32)],
        compiler_params=_params("parallel", "arbitrary", "arbitrary"),
        name="ffn_ln",
    )(x, w_gate, w_up, w_down, combine, g.reshape(1, d), b.reshape(1, d))


def _rel_bucket(dist):
    dist = jnp.maximum(dist, 0)
    max_exact = N_BUCKETS // 2
    df = jnp.maximum(dist, 1).astype(F32)
    large = max_exact + (jnp.log(df / max_exact) / math.log(MAX_DISTANCE / max_exact)
                         * (N_BUCKETS - max_exact)).astype(jnp.int32)
    large = jnp.clip(large, 0, N_BUCKETS - 1)
    return jnp.where(dist < max_exact, dist, large)


def _bias_by_distance(rel_bias, max_dist):
    return rel_bias[_rel_bucket(jnp.arange(max_dist + 1))].T.astype(F32)


def _half_masks(shape):
    lane = lax.broadcasted_iota(jnp.int32, shape, 1)
    return lane < HEAD_DIM, lane >= HEAD_DIM


def _half_ones(shape):
    lo, hi = _half_masks(shape)
    return jnp.where(lo, 1.0, 0.0).astype(BF16), jnp.where(hi, 1.0, 0.0).astype(BF16)


def _dil_prompt_body(q_ref, kp_ref, kc_ref, vp_ref, vc_ref, bias_ref, o_ref, lse_ref, s_ref, p_ref, mx_ref):
    n = N_KEYS
    first = pl.program_id(2) == 0
    pen = jnp.where(first, NEG, 0.0).astype(F32)
    lo, _ = _half_masks((n, LANES))
    ones = _half_ones((n, LANES))
    for pr in range(N_HEADS // 2):
        sl = slice(pr * LANES, (pr + 1) * LANES)
        q, kp, kc = q_ref[:, sl], kp_ref[:, sl], kc_ref[:, sl]
        for half in range(2):
            h = 2 * pr + half
            qm = q * ones[half]
            s_ref[h, :, :n] = _dot_nt(qm, kp) + bias_ref[h, :, :n] + pen
            s_ref[h, :, n:] = _dot_nt(qm, kc) + bias_ref[h, :, n:]
    for h in range(N_HEADS):
        s = s_ref[h]
        mx = jnp.max(s, axis=-1, keepdims=True)
        p_ref[h] = jnp.exp(s - mx).astype(BF16)
        mx_ref[h] = jnp.broadcast_to(mx, (n, LANES))
    for pr in range(N_HEADS // 2):
        sl = slice(pr * LANES, (pr + 1) * LANES)
        vp, vc = vp_ref[:, sl], vc_ref[:, sl]
        accs = []
        for half in range(2):
            h = 2 * pr + half
            one, other = ones[half], ones[1 - half]
            accs.append(_dot(p_ref[h, :, :n], vp * one + other) + _dot(p_ref[h, :, n:], vc * one + other))
        num = jnp.where(lo, accs[0], accs[1])
        den = jnp.where(lo, pltpu.roll(accs[0], HEAD_DIM, 1), pltpu.roll(accs[1], HEAD_DIM, 1))
        o_ref[:, sl] = num / den
        lse_ref[:, sl] = jnp.where(lo, mx_ref[2 * pr], mx_ref[2 * pr + 1]) + jnp.log(den)


def _linear_dil_body(x_ref, w_ref, o_ref, sc_ref, *, dil):
    acc = _dot(x_ref[...].astype(BF16), w_ref[...])
    rows = acc.shape[0] // dil
    for cb in range(acc.shape[1] // LANES):
        lanes = slice(cb * LANES, (cb + 1) * LANES)
        sc_ref[cb] = acc[:, lanes]
        for r in range(dil):
            o_ref[0, r, :, lanes] = sc_ref[cb, pl.ds(r, rows, stride=dil), :].astype(o_ref.dtype)


def linear_dil(x, w, dil, bsz, seq, tm=512, tn=1024):
    m, k = x.shape
    n = w.shape[1]
    tm = min(tm, seq)
    assert seq % tm == 0 and n % tn == 0 and tm % (16 * dil) == 0
    nbs = seq // tm
    return pl.pallas_call(
        functools.partial(_linear_dil_body, dil=dil),
        grid=(m // tm, n // tn),
        in_specs=[pl.BlockSpec((tm, k), lambda i, j: (i, 0)),
                  pl.BlockSpec((k, tn), lambda i, j: (0, j))],
        out_specs=pl.BlockSpec((1, dil, tm // dil, tn), lambda i, j: (i // nbs, 0, i % nbs, j)),
        out_shape=jax.ShapeDtypeStruct((bsz, dil, seq // dil, n), BF16),
        scratch_shapes=[pltpu.VMEM((tn // LANES, tm, LANES), F32)],
        compiler_params=_params("parallel", "parallel"),
        name="linear_dil",
    )(x, w)


def dilated_prompt_group(qkv, bias_tile, dil, bsz, seq):
    n = N_KEYS
    nb = seq // (dil * n)
    view = qkv.reshape(bsz * seq, 3 * D_MODEL)

    def cur(which):
        return lambda b, r, i: ((b * dil + r) * nb + i, which)

    def prev(which):
        return lambda b, r, i: ((b * dil + r) * nb + jnp.maximum(i - 1, 0), which)

    blk = (n, D_MODEL)
    o, lse = pl.pallas_call(
        _dil_prompt_body,
        grid=(bsz, dil, nb),
        in_specs=[pl.BlockSpec(blk, cur(0)),
                  pl.BlockSpec(blk, prev(1)), pl.BlockSpec(blk, cur(1)),
                  pl.BlockSpec(blk, prev(2)), pl.BlockSpec(blk, cur(2)),
                  pl.BlockSpec((N_HEADS, n, 2 * n), lambda b, r, i: (0, 0, 0))],
        out_specs=[pl.BlockSpec(blk, cur(0)), pl.BlockSpec(blk, cur(0))],
        out_shape=[jax.ShapeDtypeStruct((bsz * seq, D_MODEL), F32)] * 2,
        scratch_shapes=[pltpu.VMEM((N_HEADS, n, 2 * n), F32), pltpu.VMEM((N_HEADS, n, 2 * n), BF16),
                        pltpu.VMEM((N_HEADS, n, LANES), F32)],
        compiler_params=_params("parallel", "parallel", "arbitrary"),
        name="dilated_prompt_group",
    )(view, view, view, view, view, bias_tile)
    shape = (bsz, dil, seq // dil, D_MODEL)
    return o.reshape(shape), lse.reshape(shape)


def _merge_o_ln_body(o1, o2, o3, l1, l2, l3, w_ref, r_ref, g_ref, b_ref, out_ref, nat_ref):
    tm = out_ref.shape[0]
    ys = []
    for cb in range(D_MODEL // LANES):
        lanes = slice(cb * LANES, (cb + 1) * LANES)
        vals = []
        for j, src in enumerate((o1, o2, o3, l1, l2, l3)):
            dil = src.shape[1]
            if dil == 1:
                vals.append(src[0, 0, :, lanes])
                continue
            for r in range(dil):
                nat_ref[j, pl.ds(r, tm // dil, stride=dil), :] = src[0, r, :, lanes]
            vals.append(nat_ref[j])
        a, b, c = vals[3:]
        mx = jnp.maximum(jnp.maximum(a, b), c)
        ea, eb, ec = jnp.exp(a - mx), jnp.exp(b - mx), jnp.exp(c - mx)
        ys.append(((ea * vals[0] + eb * vals[1] + ec * vals[2]) / (ea + eb + ec)).astype(BF16))
    sub = _dot(jnp.concatenate(ys, axis=1), w_ref[...])
    out_ref[...] = _layer_norm(ALPHA * r_ref[...] + sub, g_ref[...], b_ref[...])


def merge_o_ln(outs, lses, w_o, resid, g, b, bsz, seq, tm=256):
    m, d = resid.shape
    tm = min(tm, seq)
    assert seq % tm == 0
    nbs = seq // tm
    row = pl.BlockSpec((tm, d), lambda i: (i, 0))
    vec = pl.BlockSpec((1, d), lambda i: (0, 0))

    def grp(a):
        dil = a.shape[1]
        assert tm % (8 * dil) == 0
        return pl.BlockSpec((1, dil, tm // dil, d), lambda i: (i // nbs, 0, i % nbs, 0))

    return pl.pallas_call(
        _merge_o_ln_body,
        grid=(m // tm,),
        in_specs=[grp(a) for a in (*outs, *lses)] + [pl.BlockSpec((d, d), lambda i: (0, 0)), row, vec, vec],
        out_specs=row,
        out_shape=jax.ShapeDtypeStruct((m, d), F32),
        scratch_shapes=[pltpu.VMEM((6, tm, LANES), F32)],
        compiler_params=_params("parallel"),
        name="merge_o_ln",
    )(*outs, *lses, w_o, resid, g.reshape(1, d), b.reshape(1, d))


def _dil_bias_tiles(bias_dist):
    n = N_KEYS
    step = n + np.arange(n)[:, None] - np.arange(2 * n)[None, :]
    band = (step >= 0) & (step <= n)
    tiles = []
    for _, dil in DIL_GROUPS:
        idx = np.clip(step, 0, n) * dil
        t = jnp.take(bias_dist, jnp.asarray(idx.reshape(-1), jnp.int32), axis=1).reshape(N_HEADS, n, 2 * n)
        tiles.append(jnp.where(jnp.asarray(band)[None], t, NEG))
    return tiles


SAMPLE_HEADS_PER_STEP = 4


def _bf16_round(x):
    return x.astype(BF16).astype(F32)


def _rows_to_cols(x):
    n = x.shape[1]
    r = lax.broadcasted_iota(jnp.int32, (n, n), 0)
    c = lax.broadcasted_iota(jnp.int32, (n, n), 1)
    eye = jnp.where(r == c, 1.0, 0.0).astype(BF16)
    hi, mid, lo = _split3(x)
    return _dot_nt(eye, hi) + _dot_nt(eye, mid) + _dot_nt(eye, lo)


def _dil_sample_body(q_ref, new_ref, b1_ref, b2_ref, b3_ref, a1_ref, a2_ref, a3_ref, a0_ref,
                     y_ref, o1_ref, o2_ref, o3_ref, s1_ref, s2_ref, s3_ref):
    hb = SAMPLE_HEADS_PER_STEP
    groups = ((b1_ref, a1_ref, o1_ref, s1_ref), (b2_ref, a2_ref, o2_ref, s2_ref), (b3_ref, a3_ref, o3_ref, s3_ref))
    q_rows = [_bf16_round(q_ref[0, 0, g] * ATTN_SCALE) for g in range(3)]
    q_cols = [_rows_to_cols(qr) for qr in q_rows]
    k_cols = [_rows_to_cols(new_ref[0, 0, g, 0]) for g in range(3)]
    v_cols = [_rows_to_cols(new_ref[0, 0, g, 1]) for g in range(3)]
    for g, (buf_ref, add_ref, out_ref, s_ref) in enumerate(groups):
        w = buf_ref.shape[-1]
        last = lax.broadcasted_iota(jnp.int32, (HEAD_DIM, w), 1) == w - 1
        for h in range(hb):
            k_buf, v_buf = buf_ref[0, 0, h], buf_ref[0, 1, h]
            out_ref[0, 0, h] = jnp.where(last, k_cols[g][:, h:h + 1], pltpu.roll(k_buf, w - 1, 1))
            out_ref[0, 1, h] = jnp.where(last, v_cols[g][:, h:h + 1], pltpu.roll(v_buf, w - 1, 1))
            s_ref[h:h + 1, :] = jnp.sum(_bf16_round(k_buf) * q_cols[g][:, h:h + 1], axis=0, keepdims=True)
    p_news, dens, lses = [], [], []
    for g, (buf_ref, add_ref, out_ref, s_ref) in enumerate(groups):
        s = s_ref[...] + add_ref[0]
        s_new = (jnp.sum(q_rows[g] * _bf16_round(new_ref[0, 0, g, 0]), axis=1, keepdims=True)
                 + a0_ref[0][:, :1])
        mx = jnp.maximum(jnp.max(s, axis=1, keepdims=True), s_new)
        p = jnp.exp(s - mx)
        p_new = jnp.exp(s_new - mx)
        den = jnp.sum(p, axis=1, keepdims=True) + p_new
        s_ref[...] = _bf16_round(p)
        p_news.append(_bf16_round(p_new))
        dens.append(den)
        lses.append(mx + jnp.log(den))
    mx = jnp.maximum(jnp.maximum(lses[0], lses[1]), lses[2])
    ws = [jnp.exp(l - mx) for l in lses]
    wsum = ws[0] + ws[1] + ws[2]
    coef = [ws[g] / (wsum * dens[g]) for g in range(3)]
    for h in range(hb):
        y = jnp.zeros((HEAD_DIM, 1), F32)
        for g, (buf_ref, add_ref, out_ref, s_ref) in enumerate(groups):
            acc = (jnp.sum(_bf16_round(buf_ref[0, 1, h]) * s_ref[h:h + 1, :], axis=1, keepdims=True)
                   + p_news[g][h:h + 1, :] * _bf16_round(v_cols[g][:, h:h + 1]))
            y = y + coef[g][h:h + 1, :] * acc
        y_ref[0, h] = y


def dilated_sample(qkv, bufs, bias_dist):
    bsz = qkv.shape[0]
    hb = SAMPLE_HEADS_PER_STEP
    n_hg = N_HEADS // hb
    parts = qkv.reshape(bsz, 3, 3, n_hg, hb, HEAD_DIM)
    q_rows = parts[:, :, 0].transpose(0, 2, 1, 3, 4)
    new_rows = parts[:, :, 1:3].transpose(0, 3, 1, 2, 4, 5)
    bufs_t, adds = [], []
    for buf, (window, dil) in zip(bufs, DIL_GROUPS):
        assert buf.shape[1] == window
        bufs_t.append(buf.transpose(0, 2, 3, 4, 1))
        pos = np.arange(window)
        add = jnp.take(bias_dist, jnp.asarray(window - pos, jnp.int32), axis=1)
        adds.append(jnp.where(jnp.asarray(pos % dil == 0)[None], add, NEG).reshape(n_hg, hb, window))
    add0 = jnp.broadcast_to(bias_dist[:, :1], (N_HEADS, LANES)).reshape(n_hg, hb, LANES)

    def buf_spec(window):
        return pl.BlockSpec((1, 2, hb, HEAD_DIM, window), lambda b, j: (b, 0, j, 0, 0))

    windows = [w for w, _ in DIL_GROUPS]
    res = pl.pallas_call(
        _dil_sample_body,
        grid=(bsz, n_hg),
        in_specs=[pl.BlockSpec((1, 1, 3, hb, HEAD_DIM), lambda b, j: (b, j, 0, 0, 0)),
                  pl.BlockSpec((1, 1, 3, 2, hb, HEAD_DIM), lambda b, j: (b, j, 0, 0, 0, 0))]
                 + [buf_spec(w) for w in windows]
                 + [pl.BlockSpec((1, hb, w), lambda b, j: (j, 0, 0)) for w in windows]
                 + [pl.BlockSpec((1, hb, LANES), lambda b, j: (j, 0, 0))],
        out_specs=[pl.BlockSpec((1, hb, HEAD_DIM, 1), lambda b, j: (b, j, 0, 0))] + [buf_spec(w) for w in windows],
        out_shape=[jax.ShapeDtypeStruct((bsz, N_HEADS, HEAD_DIM, 1), F32)]
                  + [jax.ShapeDtypeStruct(bt.shape, F32) for bt in bufs_t],
        scratch_shapes=[pltpu.VMEM((hb, w), F32) for w in windows],
        compiler_params=_params("parallel", "parallel"),
        name="dilated_sample",
    )(q_rows, new_rows, *bufs_t, *adds, add0)
    new_bufs = tuple(o.transpose(0, 4, 1, 2, 3) for o in res[1:])
    return res[0].reshape(bsz, D_MODEL), new_bufs


def _diff_lambda(lp, lam_init):
    a = jnp.sum(lp[0:1] * lp[1:2], axis=-1, keepdims=True)
    b = jnp.sum(lp[2:3] * lp[3:4], axis=-1, keepdims=True)
    return jnp.exp(a) - jnp.exp(b) + lam_init


def _diff_prompt_body(q_ref, k_ref, v_ref, rb_ref, lam_ref, g_ref, o_ref, bias_ref, m_ref, acc_ref,
                      *, t, n_near, lam_init):
    qi = pl.program_id(2)

    @pl.when(qi == 0)
    def _():
        for slot in range(2):
            for c in range(n_near):
                win = jnp.broadcast_to(rb_ref[0, slot:slot + 1, c * t:(c + 2) * t], (t, 2 * t))
                bias_ref[slot, c] = pltpu.roll(win, t + 1, 1, stride=1, stride_axis=0)[:, :t]

    q = q_ref[...]
    lo, hi = _half_ones(q.shape)
    qs = (q * lo, q * hi)
    m_ref[...] = jnp.full(m_ref.shape, NEG, F32)
    acc_ref[...] = jnp.zeros(acc_ref.shape, F32)
    ones = jnp.ones((t, LANES), BF16)

    def chunk(kj, c):
        start = pl.multiple_of(kj * t, t)
        k = k_ref[pl.ds(start, t), :]
        rhs = jnp.concatenate([v_ref[pl.ds(start, t), :], ones], axis=1)
        ss = [_dot_nt(qs[slot], k) for slot in range(2)]
        if c is not None:
            ss = [ss[slot] + bias_ref[slot, c] for slot in range(2)]
        m_prevs = [m_ref[slot] for slot in range(2)]
        m_news = [jnp.maximum(m_prevs[slot], jnp.max(ss[slot], axis=-1, keepdims=True)) for slot in range(2)]
        ps = [jnp.exp(ss[slot] - jnp.tile(m_news[slot], (1, t // LANES))).astype(BF16) for slot in range(2)]
        pvs = [_dot(ps[slot], rhs) for slot in range(2)]
        for slot in range(2):
            alpha = jnp.exp(m_prevs[slot] - m_news[slot])
            acc_ref[slot] = jnp.tile(alpha, (1, 2)) * acc_ref[slot] + pvs[slot]
            m_ref[slot] = m_news[slot]

    def far_body(kj, carry):
        chunk(kj, None)
        return carry

    lax.fori_loop(0, jnp.maximum(qi - (n_near - 1), 0), far_body, 0)
    for c in range(n_near):
        kj = qi - (n_near - 1) + c

        @pl.when(kj >= 0)
        def _(kj=kj, c=c):
            chunk(kj, c)

    lam = _diff_lambda(lam_ref[...], lam_init)
    o = (acc_ref[0, :, :LANES] / acc_ref[0, :, LANES:]
         - lam * (acc_ref[1, :, :LANES] / acc_ref[1, :, LANES:]))
    o = o * lax.rsqrt(jnp.mean(o * o, axis=-1, keepdims=True) + RMS_EPS)
    o_ref[...] = (o * g_ref[...] * (1.0 - lam_init)).astype(o_ref.dtype)


def diff_prompt(q, kv, rev_bias, lam_params, subln_g, lam_init, bsz, seq, t=512):
    nq = seq // t
    n_near = rev_bias.shape[-1] // t - 1
    body = functools.partial(_diff_prompt_body, t=t, n_near=n_near, lam_init=lam_init)
    return pl.pallas_call(
        body,
        grid=(bsz, H_DIFF, nq),
        in_specs=[pl.BlockSpec((t, LANES), lambda b, h, i: (b * nq + i, h)),
                  pl.BlockSpec((seq, LANES), lambda b, h, i: (b, h)),
                  pl.BlockSpec((seq, LANES), lambda b, h, i: (b, H_DIFF + h)),
                  pl.BlockSpec((1, 2, (n_near + 1) * t), lambda b, h, i: (h, 0, 0)),
                  pl.BlockSpec((4, HEAD_DIM), lambda b, h, i: (0, 0)),
                  pl.BlockSpec((1, LANES), lambda b, h, i: (0, 0))],
        out_specs=pl.BlockSpec((t, LANES), lambda b, h, i: (b * nq + i, h)),
        out_shape=jax.ShapeDtypeStruct((bsz * seq, D_MODEL), BF16),
        scratch_shapes=[pltpu.VMEM((2, n_near, t, t), F32), pltpu.VMEM((2, t, LANES), F32),
                        pltpu.VMEM((2, t, 2 * LANES), F32)],
        compiler_params=_params("parallel", "parallel", "arbitrary"),
        name="diff_prompt",
    )(q, kv, kv, rev_bias, lam_params, subln_g.reshape(1, LANES))


def _diff_rev_bias(bias_dist, t):
    n_near = -(-(FAR_DIST + t - 1) // t)
    d_max = n_near * t - 1
    assert bias_dist.shape[1] > d_max
    rel = bias_dist[:, :d_max + 1] - bias_dist[:, -1:]
    rev = jnp.concatenate([rel[:, ::-1], jnp.full((N_HEADS, t), NEG, F32)], axis=1)
    return rev.reshape(H_DIFF, 2, (n_near + 1) * t)


PAGES_PER_STEP = 8


def _diff_sample_body(pt_ref, q_ref, new_ref, *refs, lam_init):
    page_refs = refs[:PAGES_PER_STEP]
    add_ref, add0_ref, lam_ref, g_ref, o_ref, m_ref, l_ref, acc_ref = refs[PAGES_PER_STEP:]
    c = pl.program_id(1)
    rows_per_page = PAGE_SIZE * H_DIFF
    q8 = q_ref[0] * ATTN_SCALE
    lo, hi = _half_masks((H_DIFF, LANES))
    qs = jnp.concatenate([jnp.where(lo, q8, 0.0), jnp.where(hi, q8, 0.0)], axis=0).astype(BF16)

    @pl.when(c == 0)
    def _():
        m_ref[...] = jnp.full(m_ref.shape, NEG, F32)
        l_ref[...] = jnp.zeros(l_ref.shape, F32)
        acc_ref[...] = jnp.zeros(acc_ref.shape, F32)

    s = jnp.concatenate([_dot_nt(qs, pr[0, :, 0].reshape(rows_per_page, LANES).astype(BF16)) for pr in page_refs],
                        axis=-1)
    s = s + add_ref[...]
    m_prev = m_ref[...]
    m_new = jnp.maximum(m_prev, jnp.max(s, axis=-1, keepdims=True))
    alpha = jnp.exp(m_prev - m_new)
    p = jnp.exp(s - m_new[:, :1]).astype(BF16)
    l_ref[...] = alpha * l_ref[...] + jnp.sum(p.astype(F32), axis=-1, keepdims=True)
    pv = jnp.zeros((2 * H_DIFF, LANES), F32)
    for i, pr in enumerate(page_refs):
        v2 = pr[0, :, 1].reshape(rows_per_page, LANES).astype(BF16)
        pv = pv + _dot(p[:, i * rows_per_page:(i + 1) * rows_per_page], v2)
    acc_ref[...] = alpha * acc_ref[...] + pv
    m_ref[...] = m_new

    @pl.when(c == pl.num_programs(1) - 1)
    def _():
        k_new, v_new = _bf16_round(new_ref[0, 0]), _bf16_round(new_ref[0, 1])
        k16 = jnp.concatenate([k_new, k_new], axis=0)
        v16 = jnp.concatenate([v_new, v_new], axis=0)
        s_new = jnp.sum(qs.astype(F32) * k16, axis=-1, keepdims=True) + add0_ref[:, :1]
        m_prev = m_ref[...][:, :1]
        m_fin = jnp.maximum(m_prev, s_new)
        a_fin = jnp.exp(m_prev - m_fin)
        p_new = jnp.exp(s_new - m_fin)
        den = a_fin * l_ref[...][:, :1] + p_new
        res = (a_fin * acc_ref[...] + _bf16_round(p_new) * v16) / den
        lam = _diff_lambda(lam_ref[...], lam_init)
        o = res[:H_DIFF] - lam * res[H_DIFF:]
        o = o * lax.rsqrt(jnp.mean(o * o, axis=-1, keepdims=True) + RMS_EPS)
        o_ref[0] = o * g_ref[...] * (1.0 - lam_init)


def diff_sample(qkv, cache, page_table, bias_dist, lam_params, subln_g, lam_init):
    bsz, n_pages = page_table.shape
    assert n_pages % PAGES_PER_STEP == 0
    n_past = n_pages * PAGE_SIZE
    rows = 2 * H_DIFF
    order = np.array([2 * (r % H_DIFF) + r // H_DIFF for r in range(rows)])
    bias_rows = bias_dist[order]
    past = jnp.take(bias_rows, jnp.asarray(n_past - np.arange(n_past), jnp.int32), axis=1)
    own = (np.arange(rows)[:, None] % H_DIFF) == np.arange(H_DIFF)[None, :]
    add = jnp.where(jnp.asarray(own)[:, None, :], past[:, :, None], NEG).reshape(rows, n_past * H_DIFF)
    add0 = jnp.broadcast_to(bias_rows[:, :1], (rows, LANES))
    step_lanes = PAGES_PER_STEP * PAGE_SIZE * H_DIFF

    def page_spec(i):
        return pl.BlockSpec((1, PAGE_SIZE, 2, H_DIFF, LANES),
                            lambda b, c, pt: (pt[b * n_pages + c * PAGES_PER_STEP + i], 0, 0, 0, 0))

    grid_spec = pltpu.PrefetchScalarGridSpec(
        num_scalar_prefetch=1,
        grid=(bsz, n_pages // PAGES_PER_STEP),
        in_specs=[pl.BlockSpec((1, H_DIFF, LANES), lambda b, c, pt: (b, 0, 0)),
                  pl.BlockSpec((1, 2, H_DIFF, LANES), lambda b, c, pt: (b, 0, 0, 0))]
                 + [page_spec(i) for i in range(PAGES_PER_STEP)]
                 + [pl.BlockSpec((rows, step_lanes), lambda b, c, pt: (0, c)),
                    pl.BlockSpec((rows, LANES), lambda b, c, pt: (0, 0)),
                    pl.BlockSpec((4, HEAD_DIM), lambda b, c, pt: (0, 0)),
                    pl.BlockSpec((1, LANES), lambda b, c, pt: (0, 0))],
        out_specs=pl.BlockSpec((1, H_DIFF, LANES), lambda b, c, pt: (b, 0, 0)),
        scratch_shapes=[pltpu.VMEM((rows, LANES), F32), pltpu.VMEM((rows, LANES), F32),
                        pltpu.VMEM((rows, LANES), F32)],
    )
    out = pl.pallas_call(
        functools.partial(_diff_sample_body, lam_init=lam_init),
        grid_spec=grid_spec,
        out_shape=jax.ShapeDtypeStruct((bsz, H_DIFF, LANES), F32),
        compiler_params=_params("parallel", "arbitrary"),
        name="diff_sample",
    )(page_table.reshape(-1), qkv[:, :D_MODEL].reshape(bsz, H_DIFF, LANES),
      qkv[:, D_MODEL:].reshape(bsz, 2, H_DIFF, LANES), *([cache] * PAGES_PER_STEP),
      add, add0, lam_params, subln_g.reshape(1, LANES))
    return out.reshape(bsz, D_MODEL)


def _split3(x):
    hi = x.astype(BF16)
    r1 = x - hi.astype(F32)
    mid = r1.astype(BF16)
    lo = (r1 - mid.astype(F32)).astype(BF16)
    return hi, mid, lo


def _tri_cumsum(x, tri):
    hi, mid, lo = _split3(x)
    return _dot(tri, hi) + _dot(tri, mid) + _dot(tri, lo)


def _lower_tri(n):
    r = lax.broadcasted_iota(jnp.int32, (n, n), 0)
    c = lax.broadcasted_iota(jnp.int32, (n, n), 1)
    return jnp.where(r >= c, 1.0, 0.0).astype(BF16)


def _cumsum_body(x_ref, o_ref, carry_ref, *, chunk):
    tri = _lower_tri(chunk)
    carry_ref[...] = jnp.zeros(carry_ref.shape, F32)

    def body(i, c):
        start = pl.multiple_of(i * chunk, chunk)
        cum = _tri_cumsum(x_ref[0, pl.ds(start, chunk), :], tri) + carry_ref[...]
        o_ref[0, pl.ds(start, chunk), :] = cum
        carry_ref[...] = cum[chunk - 1:chunk, :]
        return c

    lax.fori_loop(0, x_ref.shape[1] // chunk, body, 0)


def seq_cumsum(x, chunk=128):
    bsz, seq, c = x.shape
    assert seq % chunk == 0
    return pl.pallas_call(
        functools.partial(_cumsum_body, chunk=chunk),
        grid=(bsz,),
        in_specs=[pl.BlockSpec((1, seq, c), lambda b: (b, 0, 0))],
        out_specs=pl.BlockSpec((1, seq, c), lambda b: (b, 0, 0)),
        out_shape=jax.ShapeDtypeStruct((bsz, seq, c), F32),
        scratch_shapes=[pltpu.VMEM((1, c), F32)],
        compiler_params=_params("parallel"),
        name="seq_cumsum",
    )(x)


def _fox_prompt_body(q_ref, k_ref, v_ref, ccol_ref, crow_ref, o_ref, m_ref, acc_ref, *, t):
    qi = pl.program_id(2)
    q = q_ref[...]
    ones = _half_ones(q.shape)
    qs = [q * one for one in ones]
    m_ref[...] = jnp.full(m_ref.shape, NEG, F32)
    acc_ref[...] = jnp.zeros(acc_ref.shape, F32)
    cq = ccol_ref[0, 0]
    cref = [cq[0:1, s:s + 1] for s in range(2)]
    a_rep = [jnp.broadcast_to(cq[:, s:s + 1] - cref[s], (t, LANES)) for s in range(2)]

    def chunk(kj, diag):
        start = pl.multiple_of(kj * t, t)
        k = k_ref[pl.ds(start, t), :]
        v = v_ref[pl.ds(start, t), :]
        crow = crow_ref[0, 0, :, pl.ds(start, t)]
        ss = [_dot_nt(qs[slot], k) + jnp.tile(a_rep[slot], (1, t // LANES))
              - (crow[slot:slot + 1, :] - cref[slot]) for slot in range(2)]
        if diag:
            r = lax.broadcasted_iota(jnp.int32, (t, t), 0)
            c = lax.broadcasted_iota(jnp.int32, (t, t), 1)
            ss = [jnp.where(r >= c, s, NEG) for s in ss]
        m_prevs = [m_ref[slot] for slot in range(2)]
        m_news = [jnp.maximum(m_prevs[slot], jnp.max(ss[slot], axis=-1, keepdims=True)) for slot in range(2)]
        ps = [jnp.exp(ss[slot] - jnp.tile(m_news[slot], (1, t // LANES))).astype(BF16) for slot in range(2)]
        pvs = [_dot(ps[slot], v * ones[slot] + ones[1 - slot]) for slot in range(2)]
        for slot in range(2):
            acc_ref[slot] = jnp.exp(m_prevs[slot] - m_news[slot]) * acc_ref[slot] + pvs[slot]
            m_ref[slot] = m_news[slot]

    def body(kj, carry):
        chunk(kj, False)
        return carry

    lax.fori_loop(0, qi, body, 0)
    chunk(qi, True)
    lo, _ = _half_masks((t, LANES))
    num = jnp.where(lo, acc_ref[0], acc_ref[1])
    den = jnp.where(lo, pltpu.roll(acc_ref[0], HEAD_DIM, 1), pltpu.roll(acc_ref[1], HEAD_DIM, 1))
    o_ref[...] = (num / den).astype(o_ref.dtype)


def fox_prompt(q, kv, cum, bsz, seq, t=512):
    nq = seq // t
    n_pair = N_HEADS // 2
    ccol = cum.reshape(bsz, seq, n_pair, 2).transpose(0, 2, 1, 3)
    crow = cum.reshape(bsz, seq, n_pair, 2).transpose(0, 2, 3, 1)
    return pl.pallas_call(
        functools.partial(_fox_prompt_body, t=t),
        grid=(bsz, n_pair, nq),
        in_specs=[pl.BlockSpec((t, LANES), lambda b, h, i: (b * nq + i, h)),
                  pl.BlockSpec((seq, LANES), lambda b, h, i: (b, h)),
                  pl.BlockSpec((seq, LANES), lambda b, h, i: (b, n_pair + h)),
                  pl.BlockSpec((1, 1, t, 2), lambda b, h, i: (b, h, i, 0)),
                  pl.BlockSpec((1, 1, 2, seq), lambda b, h, i: (b, h, 0, 0))],
        out_specs=pl.BlockSpec((t, LANES), lambda b, h, i: (b * nq + i, h)),
        out_shape=jax.ShapeDtypeStruct((bsz * seq, D_MODEL), BF16),
        scratch_shapes=[pltpu.VMEM((2, t, LANES), F32), pltpu.VMEM((2, t, LANES), F32)],
        compiler_params=_params("parallel", "parallel", "arbitrary"),
        name="fox_prompt",
    )(q, kv, kv, ccol, crow)


def _fox_sample_body(pt_ref, q_ref, new_ref, lfn_ref, *refs, n_pages):
    kv_refs = refs[:PAGES_PER_STEP]
    lf_refs = refs[PAGES_PER_STEP:PAGES_PER_STEP + n_pages]
    y_ref, dec_ref, s_ref, m_ref, l_ref, acc_ref = refs[PAGES_PER_STEP + n_pages:]
    c = pl.program_id(1)
    width = PAGES_PER_STEP * PAGE_SIZE

    @pl.when(c == 0)
    def _():
        r = lax.broadcasted_iota(jnp.int32, (PAGE_SIZE, PAGE_SIZE), 0)
        col = lax.broadcasted_iota(jnp.int32, (PAGE_SIZE, PAGE_SIZE), 1)
        upper = jnp.where(r > col, 1.0, 0.0).astype(BF16)
        carry = lfn_ref[0]
        for i in reversed(range(n_pages)):
            x = lf_refs[i][0]
            hi, mid, lo = _split3(x)
            dec_ref[:, i * PAGE_SIZE:(i + 1) * PAGE_SIZE] = _dot(hi, upper) + _dot(mid, upper) + _dot(lo, upper) + carry
            carry = carry + jnp.sum(x, axis=1, keepdims=True)
        m_ref[...] = jnp.full(m_ref.shape, NEG, F32)
        l_ref[...] = jnp.zeros(l_ref.shape, F32)
        acc_ref[...] = jnp.zeros(acc_ref.shape, F32)

    q_cols = _bf16_round(_rows_to_cols(q_ref[0]) * ATTN_SCALE)
    qs = [q_cols[:, h:h + 1] for h in range(N_HEADS)]
    for i in range(PAGES_PER_STEP):
        for h in range(N_HEADS):
            s_ref[h:h + 1, i * PAGE_SIZE:(i + 1) * PAGE_SIZE] = jnp.sum(
                _bf16_round(kv_refs[i][0, 0, h]) * qs[h], axis=0, keepdims=True)
    s = s_ref[...] + dec_ref[:, pl.ds(pl.multiple_of(c * width, width), width)]
    m_prev = m_ref[...]
    m_new = jnp.maximum(m_prev, jnp.max(s, axis=1, keepdims=True))
    alpha = jnp.exp(m_prev - m_new)
    p = jnp.exp(s - m_new[:, :1])
    l_ref[...] = alpha * l_ref[...] + jnp.sum(p, axis=1, keepdims=True)
    m_ref[...] = m_new
    s_ref[...] = _bf16_round(p)
    for h in range(N_HEADS):
        acc = alpha[h:h + 1, :] * acc_ref[h]
        for i in range(PAGES_PER_STEP):
            acc = acc + _bf16_round(kv_refs[i][0, 1, h]) * s_ref[h:h + 1, i * PAGE_SIZE:(i + 1) * PAGE_SIZE]
        acc_ref[h] = acc

    @pl.when(c == pl.num_programs(1) - 1)
    def _():
        k_cols = _bf16_round(_rows_to_cols(new_ref[0, 0]))
        v_cols = _bf16_round(_rows_to_cols(new_ref[0, 1]))
        for h in range(N_HEADS):
            k_new, v_new = k_cols[:, h:h + 1], v_cols[:, h:h + 1]
            s_new = jnp.sum(qs[h] * k_new, axis=0, keepdims=True)
            m_h, l_h = m_ref[h:h + 1, :1], l_ref[h:h + 1, :1]
            m_fin = jnp.maximum(m_h, s_new)
            a_fin = jnp.exp(m_h - m_fin)
            p_new = jnp.exp(s_new - m_fin)
            num = a_fin * jnp.sum(acc_ref[h], axis=1, keepdims=True) + _bf16_round(p_new) * v_new
            y_ref[0, h] = num / (a_fin * l_h + p_new)


def fox_sample(qkv, cache_kv, cache_logf, page_table, logf_new):
    bsz, n_pages = page_table.shape
    assert n_pages % PAGES_PER_STEP == 0
    kv_t = cache_kv.transpose(0, 2, 3, 4, 1)
    lf_t = cache_logf.transpose(0, 2, 1)
    q_rows = qkv[:, :D_MODEL].reshape(bsz, N_HEADS, HEAD_DIM)
    new_rows = qkv[:, D_MODEL:].reshape(bsz, 2, N_HEADS, HEAD_DIM)

    def kv_spec(i):
        return pl.BlockSpec((1, 2, N_HEADS, HEAD_DIM, PAGE_SIZE),
                            lambda b, c, pt: (pt[b * n_pages + c * PAGES_PER_STEP + i], 0, 0, 0, 0))

    def lf_spec(i):
        return pl.BlockSpec((1, N_HEADS, PAGE_SIZE), lambda b, c, pt: (pt[b * n_pages + i], 0, 0))

    past = n_pages * PAGE_SIZE
    grid_spec = pltpu.PrefetchScalarGridSpec(
        num_scalar_prefetch=1,
        grid=(bsz, n_pages // PAGES_PER_STEP),
        in_specs=[pl.BlockSpec((1, N_HEADS, HEAD_DIM), lambda b, c, pt: (b, 0, 0)),
                  pl.BlockSpec((1, 2, N_HEADS, HEAD_DIM), lambda b, c, pt: (b, 0, 0, 0)),
                  pl.BlockSpec((1, N_HEADS, 1), lambda b, c, pt: (b, 0, 0))]
                 + [kv_spec(i) for i in range(PAGES_PER_STEP)] + [lf_spec(i) for i in range(n_pages)],
        out_specs=pl.BlockSpec((1, N_HEADS, HEAD_DIM, 1), lambda b, c, pt: (b, 0, 0, 0)),
        scratch_shapes=[pltpu.VMEM((N_HEADS, past), F32),
                        pltpu.VMEM((N_HEADS, PAGES_PER_STEP * PAGE_SIZE), F32),
                        pltpu.VMEM((N_HEADS, LANES), F32), pltpu.VMEM((N_HEADS, LANES), F32),
                        pltpu.VMEM((N_HEADS, HEAD_DIM, PAGE_SIZE), F32)],
    )
    out = pl.pallas_call(
        functools.partial(_fox_sample_body, n_pages=n_pages),
        grid_spec=grid_spec,
        out_shape=jax.ShapeDtypeStruct((bsz, N_HEADS, HEAD_DIM, 1), F32),
        compiler_params=_params("parallel", "arbitrary"),
        name="fox_sample",
    )(page_table.reshape(-1), q_rows, new_rows, logf_new.reshape(bsz, N_HEADS, 1),
      *([kv_t] * PAGES_PER_STEP), *([lf_t] * n_pages))
    return out.reshape(bsz, D_MODEL)


def _pad_cols(w, n):
    return jnp.pad(w, [(0, 0)] * (w.ndim - 1) + [(0, n - w.shape[-1])])


def _split_gu(w_gu):
    gate, up = w_gu[..., :D_FF], w_gu[..., D_FF:]
    return _pad_cols(gate, D_FF_PAD).astype(BF16), _pad_cols(up, D_FF_PAD).astype(BF16)


def _pad_down(w_down):
    pad = [(0, 0)] * (w_down.ndim - 2) + [(0, D_FF_PAD - D_FF), (0, 0)]
    return jnp.pad(w_down, pad).astype(BF16)


def _mixer_a(hp, hs, bufs, w_qkv, w_o, ln_g, ln_b, bias_dist, bsz, seq):
    w = w_qkv.astype(BF16)
    col = np.arange(w.shape[1]) % (3 * D_MODEL) < D_MODEL
    wq_scaled = jnp.where(jnp.asarray(col)[None, :], w * ATTN_SCALE, w)
    (qkv_s,) = linear(hs, w, [F32])

    tail = min(DIL_GROUPS[-1][0], seq)
    kv_cols = np.concatenate([np.arange(g * 3 * D_MODEL + D_MODEL, (g + 1) * 3 * D_MODEL) for g in range(3)])
    x_tail = hp.reshape(bsz, seq, D_MODEL)[:, seq - tail:].reshape(bsz * tail, D_MODEL)
    (kv_tail,) = linear(x_tail, w[:, kv_cols], [F32])
    kv_tail = kv_tail.reshape(bsz, tail, 3, 2, N_HEADS, HEAD_DIM)
    bufs_p = tuple(kv_tail[:, tail - min(wd, seq):, g] for g, (wd, _) in enumerate(DIL_GROUPS))

    tiles = _dil_bias_tiles(bias_dist)
    outs, lses = [], []
    for g, (_, dil) in enumerate(DIL_GROUPS):
        qkv_g = linear_dil(hp, wq_scaled[:, g * 3 * D_MODEL:(g + 1) * 3 * D_MODEL], dil, bsz, seq)
        o, l = dilated_prompt_group(qkv_g, tiles[g], dil, bsz, seq)
        outs.append(o)
        lses.append(l)
    hp = merge_o_ln(outs, lses, w_o, hp, ln_g, ln_b, bsz, seq)

    ys, bufs_s = dilated_sample(qkv_s, bufs, bias_dist)
    hs = linear_ln(ys, w_o, hs, ln_g, ln_b)
    return hp, hs, bufs_p, bufs_s


def _mixer_b(hp, hs, cache_kv, page_table, w_qkv, lam_params, subln_g, w_o, ln_g, ln_b, bias_dist,
             layer_idx, bsz, seq):
    w = w_qkv.astype(BF16)
    lam_init = 0.8 - 0.6 * math.exp(-0.3 * layer_idx)
    (q_p,) = linear(hp, w[:, :D_MODEL], [BF16], scale=ATTN_SCALE)
    kv_p, kv_pb = linear(hp, w[:, D_MODEL:], [F32, BF16])
    (qkv_s,) = linear(hs, w, [F32])
    t = min(512, seq)
    o = diff_prompt(q_p, kv_pb, _diff_rev_bias(bias_dist, t), lam_params, subln_g, lam_init, bsz, seq, t)
    hp = linear_ln(o, w_o, hp, ln_g, ln_b)

    dec = hs.shape[0]
    ys = diff_sample(qkv_s, cache_kv, page_table, bias_dist, lam_params, subln_g, lam_init)
    hs = linear_ln(ys, w_o, hs, ln_g, ln_b)
    st_p = kv_p.reshape(bsz, seq, 2, H_DIFF, 2 * HEAD_DIM)
    st_s = qkv_s[:, D_MODEL:].reshape(dec, 1, 2, H_DIFF, 2 * HEAD_DIM)
    return hp, hs, st_p, st_s


def _mixer_c(hp, hs, cache_kv, cache_logf, page_table, w_qkv, w_f, b_f, w_o, ln_g, ln_b, bsz, seq):
    w = w_qkv.astype(BF16)
    wf = _pad_cols(w_f, LANES).astype(BF16)
    bf = _pad_cols(b_f.reshape(1, N_HEADS), LANES).astype(F32)
    (q_p,) = linear(hp, w[:, :D_MODEL], [BF16], scale=ATTN_SCALE)
    kv_p, kv_pb = linear(hp, w[:, D_MODEL:], [F32, BF16])
    (qkv_s,) = linear(hs, w, [F32])
    logf_p = forget_log(hp, wf, bf)[:, :N_HEADS].reshape(bsz, seq, N_HEADS)
    logf_s = forget_log(hs, wf, bf)[:, :N_HEADS]
    cum = seq_cumsum(logf_p)
    o = fox_prompt(q_p, kv_pb, cum, bsz, seq)
    hp = linear_ln(o, w_o, hp, ln_g, ln_b)

    dec = hs.shape[0]
    ys = fox_sample(qkv_s, cache_kv, cache_logf, page_table, logf_s)
    hs = linear_ln(ys, w_o, hs, ln_g, ln_b)
    st_p = (kv_p.reshape(bsz, seq, 2, N_HEADS, HEAD_DIM), logf_p)
    st_s = (qkv_s[:, D_MODEL:].reshape(dec, 1, 2, N_HEADS, HEAD_DIM), logf_s.reshape(dec, 1, N_HEADS))
    return hp, hs, st_p, st_s


def _dense_ffn(h, w_gu, w_down, ln_g, ln_b):
    gate, up = _split_gu(w_gu)
    return ffn_ln(h, gate[None], up[None], _pad_down(w_down)[None], None, ln_g, ln_b)


def _moe_ffn(h, w_router, w_gu, w_down, ln_g, ln_b):
    combine = moe_router(h, _pad_cols(w_router, LANES).astype(BF16))
    gate, up = _split_gu(w_gu)
    return ffn_ln(h, gate, up, _pad_down(w_down), combine, ln_g, ln_b)


def kernel(x_prompt, x_sample, cache_l0_kv_w128, cache_l0_kv_w512, cache_l0_kv_w2048, cache_l1_kv, cache_l2_kv, cache_l2_logf, cache_l3_kv_w128, cache_l3_kv_w512, cache_l3_kv_w2048, page_table, p_prompt, p_sample, rel_bias, ln_g, ln_b, ple_w_in, ple_w_gate, ple_b_gate, a_w_qkv, a_w_o, b_w_qkv, b_lambda, b_subln_g, b_w_o, c_w_qkv, c_w_f, c_b_f, c_w_o, ffn_w_gu, ffn_w_down, moe_router, moe_w_gu, moe_w_down):
    bsz, seq, d = x_prompt.shape
    dec = x_sample.shape[0]
    assert d == D_MODEL and x_sample.shape[1] == 1 and seq % DIL_GROUPS[-1][0] == 0
    caches = ((cache_l0_kv_w128, cache_l0_kv_w512, cache_l0_kv_w2048), (cache_l1_kv,),
              (cache_l2_kv, cache_l2_logf), (cache_l3_kv_w128, cache_l3_kv_w512, cache_l3_kv_w2048))
    bias_dist = _bias_by_distance(rel_bias, max(MAX_DISTANCE, page_table.shape[1] * PAGE_SIZE))
    hp = x_prompt.reshape(bsz * seq, d)
    hs = x_sample.reshape(dec, d)
    states = []
    for i in range(DEPTH):
        kind, j = i % 3, i // 3
        g0, b0, g1, b1 = ln_g[i, 0], ln_b[i, 0], ln_g[i, 1], ln_b[i, 1]
        if kind == 0:
            hp, hs, st_p, st_s = _mixer_a(hp, hs, caches[i], a_w_qkv[j], a_w_o[j].astype(BF16), g0, b0,
                                          bias_dist, bsz, seq)
        elif kind == 1:
            hp, hs, st_p, st_s = _mixer_b(hp, hs, caches[i][0], page_table, b_w_qkv[j], b_lambda[j],
                                          b_subln_g[j], b_w_o[j].astype(BF16), g0, b0, bias_dist, i, bsz, seq)
            st_p, st_s = (st_p,), (st_s,)
        else:
            hp, hs, st_p, st_s = _mixer_c(hp, hs, caches[i][0], caches[i][1], page_table, c_w_qkv[j],
                                          c_w_f[j], c_b_f[j], c_w_o[j].astype(BF16), g0, b0, bsz, seq)
        states.append((st_p, st_s))
        f = i // 2
        if i % 2 == 0:
            hp = _dense_ffn(hp, ffn_w_gu[f], ffn_w_down[f], g1, b1)
            hs = _dense_ffn(hs, ffn_w_gu[f], ffn_w_down[f], g1, b1)
        else:
            hp = _moe_ffn(hp, moe_router[f], moe_w_gu[f], moe_w_down[f], g1, b1)
            hs = _moe_ffn(hs, moe_router[f], moe_w_gu[f], moe_w_down[f], g1, b1)
        wg, wi = ple_w_gate[i].astype(BF16), ple_w_in[i].astype(BF16)
        hp = per_layer_embed(hp, p_prompt[i].reshape(bsz * seq, D_PLE), wg, ple_b_gate[i], wi)
        hs = per_layer_embed(hs, p_sample[i].reshape(dec, D_PLE), wg, ple_b_gate[i], wi)
    (l0_p, l0_s), (l1_p, l1_s), (l2_p, l2_s), (l3_p, l3_s) = states
    return (hp.reshape(bsz, seq, d), hs.reshape(dec, 1, d),
            l0_p[0], l0_s[0], l0_p[1], l0_s[1], l0_p[2], l0_s[2],
            l1_p[0], l1_s[0],
            l2_p[0], l2_s[0], l2_p[1], l2_s[1],
            l3_p[0], l3_s[0], l3_p[1], l3_s[1], l3_p[2], l3_s[2])
```

```python
import functools
import math

import numpy as np
import jax
import jax.numpy as jnp
from jax import lax
from jax.experimental import pallas as pl
from jax.experimental.pallas import tpu as pltpu

F32 = jnp.float32
BF16 = jnp.bfloat16

D_MODEL = 1024
HEAD_DIM = 64
N_HEADS = 16
H_DIFF = 8
DIL_GROUPS = ((128, 1), (512, 4), (2048, 16))
N_KEYS = 128
N_BUCKETS = 32
MAX_DISTANCE = 2048
D_FF = 2752
D_FF_PAD = 2816
N_EXPERTS = 8
D_PLE = 256
DEPTH = 4
PAGE_SIZE = 128
LN_EPS = 1e-5
RMS_EPS = 1e-5
ALPHA = (2 * DEPTH) ** 0.25
ATTN_SCALE = HEAD_DIM ** -0.5
NEG = -1e30
LANES = 128
FAR_DIST = 1520
VMEM_LIMIT = 56 * 1024 * 1024


def _params(*sem):
    return pltpu.CompilerParams(dimension_semantics=sem, vmem_limit_bytes=VMEM_LIMIT)


def _dot(a, b):
    return jnp.dot(a, b, preferred_element_type=F32)


def _dot_nt(a, b):
    return lax.dot_general(a, b, (((1,), (1,)), ((), ())), preferred_element_type=F32)


def _layer_norm(y, g, b):
    mu = jnp.mean(y, axis=-1, keepdims=True)
    yc = y - mu
    var = jnp.mean(yc * yc, axis=-1, keepdims=True)
    return yc * lax.rsqrt(var + LN_EPS) * g + b


def _sigmoid(x):
    return 1.0 / (1.0 + jnp.exp(-x))


def _linear_body(x_ref, w_ref, *o_refs, scale):
    acc = _dot(x_ref[...].astype(BF16), w_ref[...])
    if scale is not None:
        acc = acc * scale
    for o_ref in o_refs:
        o_ref[...] = acc.astype(o_ref.dtype)


def linear(x, w, out_dtypes, scale=None, tm=512, tn=1024):
    m, k = x.shape
    n = w.shape[1]
    tm, tn = min(tm, m), min(tn, n)
    assert m % tm == 0 and n % tn == 0
    outs = pl.pallas_call(
        functools.partial(_linear_body, scale=scale),
        grid=(m // tm, n // tn),
        in_specs=[pl.BlockSpec((tm, k), lambda i, j: (i, 0)),
                  pl.BlockSpec((k, tn), lambda i, j: (0, j))],
        out_specs=[pl.BlockSpec((tm, tn), lambda i, j: (i, j)) for _ in out_dtypes],
        out_shape=[jax.ShapeDtypeStruct((m, n), d) for d in out_dtypes],
        compiler_params=_params("parallel", "parallel"),
        name="linear",
    )(x, w)
    return outs


def _forget_body(x_ref, w_ref, b_ref, o_ref):
    z = _dot(x_ref[...].astype(BF16), w_ref[...]) + b_ref[...]
    o_ref[...] = -(jnp.maximum(-z, 0.0) + jnp.log1p(jnp.exp(-jnp.abs(z))))


def forget_log(x, w_pad, b_pad, tm=512):
    m, k = x.shape
    tm = min(tm, m)
    assert m % tm == 0
    return pl.pallas_call(
        _forget_body,
        grid=(m // tm,),
        in_specs=[pl.BlockSpec((tm, k), lambda i: (i, 0)),
                  pl.BlockSpec((k, LANES), lambda i: (0, 0)),
                  pl.BlockSpec((1, LANES), lambda i: (0, 0))],
        out_specs=pl.BlockSpec((tm, LANES), lambda i: (i, 0)),
        out_shape=jax.ShapeDtypeStruct((m, LANES), F32),
        compiler_params=_params("parallel"),
        name="forget_log",
    )(x, w_pad, b_pad)


def _linear_ln_body(a_ref, w_ref, r_ref, g_ref, b_ref, o_ref):
    sub = _dot(a_ref[...].astype(BF16), w_ref[...])
    o_ref[...] = _layer_norm(ALPHA * r_ref[...] + sub, g_ref[...], b_ref[...])


def linear_ln(a, w, resid, g, b, tm=512):
    m, k = a.shape
    n = w.shape[1]
    tm = min(tm, m)
    assert m % tm == 0
    return pl.pallas_call(
        _linear_ln_body,
        grid=(m // tm,),
        in_specs=[pl.BlockSpec((tm, k), lambda i: (i, 0)),
                  pl.BlockSpec((k, n), lambda i: (0, 0)),
                  pl.BlockSpec((tm, n), lambda i: (i, 0)),
                  pl.BlockSpec((1, n), lambda i: (0, 0)),
                  pl.BlockSpec((1, n), lambda i: (0, 0))],
        out_specs=pl.BlockSpec((tm, n), lambda i: (i, 0)),
        out_shape=jax.ShapeDtypeStruct((m, n), F32),
        compiler_params=_params("parallel"),
        name="linear_ln",
    )(a, w, resid, g.reshape(1, n), b.reshape(1, n))


def _ple_body(x_ref, p_ref, wg_ref, bg_ref, wp_ref, o_ref):
    x = x_ref[...]
    gate = _sigmoid(_dot(x.astype(BF16), wg_ref[...]) + bg_ref[...])
    o_ref[...] = x + gate * _dot(p_ref[...].astype(BF16), wp_ref[...])


def per_layer_embed(x, p, w_gate, b_gate, w_in, tm=512):
    m, d = x.shape
    dp = p.shape[1]
    tm = min(tm, m)
    assert m % tm == 0
    return pl.pallas_call(
        _ple_body,
        grid=(m // tm,),
        in_specs=[pl.BlockSpec((tm, d), lambda i: (i, 0)),
                  pl.BlockSpec((tm, dp), lambda i: (i, 0)),
                  pl.BlockSpec((d, d), lambda i: (0, 0)),
                  pl.BlockSpec((1, d), lambda i: (0, 0)),
                  pl.BlockSpec((dp, d), lambda i: (0, 0))],
        out_specs=pl.BlockSpec((tm, d), lambda i: (i, 0)),
        out_shape=jax.ShapeDtypeStruct((m, d), F32),
        compiler_params=_params("parallel"),
        name="per_layer_embed",
    )(x, p, w_gate, b_gate.reshape(1, d), w_in)


def _router_body(x_ref, w_ref, o_ref):
    logits = _dot(x_ref[...].astype(BF16), w_ref[...])
    lane = lax.broadcasted_iota(jnp.int32, logits.shape, 1)
    logits = jnp.where(lane < N_EXPERTS, logits, NEG)
    v1 = jnp.max(logits, axis=-1, keepdims=True)
    i1 = jnp.min(jnp.where(logits == v1, lane, LANES), axis=-1, keepdims=True)
    rest = jnp.where(lane == i1, NEG, logits)
    v2 = jnp.max(rest, axis=-1, keepdims=True)
    i2 = jnp.min(jnp.where(rest == v2, lane, LANES), axis=-1, keepdims=True)
    e2 = jnp.exp(v2 - v1)
    g1 = 1.0 / (1.0 + e2)
    g2 = e2 / (1.0 + e2)
    o_ref[...] = jnp.where(lane == i1, g1, 0.0) + jnp.where(lane == i2, g2, 0.0)


def moe_router(x, w_pad, tm=512):
    m, k = x.shape
    tm = min(tm, m)
    assert m % tm == 0
    return pl.pallas_call(
        _router_body,
        grid=(m // tm,),
        in_specs=[pl.BlockSpec((tm, k), lambda i: (i, 0)),
                  pl.BlockSpec((k, LANES), lambda i: (0, 0))],
        out_specs=pl.BlockSpec((tm, LANES), lambda i: (i, 0)),
        out_shape=jax.ShapeDtypeStruct((m, LANES), F32),
        compiler_params=_params("parallel"),
        name="moe_router",
    )(x, w_pad)


def _ffn_body(x_ref, wg_ref, wu_ref, wd_ref, c_ref, g_ref, b_ref, o_ref, xb_ref, acc_ref, *, gated):
    e, f = pl.program_id(1), pl.program_id(2)

    @pl.when((e == 0) & (f == 0))
    def _():
        xb_ref[...] = x_ref[...].astype(BF16)
        acc_ref[...] = jnp.zeros_like(acc_ref)

    xb = xb_ref[...]
    gate = _dot(xb, wg_ref[0])
    up = _dot(xb, wu_ref[0])
    h = (gate * _sigmoid(gate) * up).astype(BF16)
    y = _dot(h, wd_ref[0])
    if gated:
        c = c_ref[...]
        lane = lax.broadcasted_iota(jnp.int32, c.shape, 1)
        y = y * jnp.sum(jnp.where(lane == e, c, 0.0), axis=-1, keepdims=True)
    acc_ref[...] += y

    @pl.when((e == pl.num_programs(1) - 1) & (f == pl.num_programs(2) - 1))
    def _():
        o_ref[...] = _layer_norm(ALPHA * x_ref[...] + acc_ref[...], g_ref[...], b_ref[...])


def ffn_ln(x, w_gate, w_up, w_down, combine, g, b, tm=1024, tf=256):
    m, d = x.shape
    n_e, _, f_pad = w_gate.shape
    tm = min(tm, m)
    assert m % tm == 0 and f_pad % tf == 0
    gated = combine is not None
    if not gated:
        combine = jnp.zeros((m, LANES), F32)
    return pl.pallas_call(
        functools.partial(_ffn_body, gated=gated),
        grid=(m // tm, n_e, f_pad // tf),
        in_specs=[pl.BlockSpec((tm, d), lambda i, e, f: (i, 0)),
                  pl.BlockSpec((1, d, tf), lambda i, e, f: (e, 0, f)),
                  pl.BlockSpec((1, d, tf), lambda i, e, f: (e, 0, f)),
                  pl.BlockSpec((1, tf, d), lambda i, e, f: (e, f, 0)),
                  pl.BlockSpec((tm, LANES), lambda i, e, f: (i, 0)),
                  pl.BlockSpec((1, d), lambda i, e, f: (0, 0)),
                  pl.BlockSpec((1, d), lambda i, e, f: (0, 0))],
        out_specs=pl.BlockSpec((tm, d), lambda i, e, f: (i, 0)),
        out_shape=jax.ShapeDtypeStruct((m, d), F32),
        scratch_shapes=[pltpu.VMEM((tm, d), BF16), pltpu.VMEM((tm, d), F32)],
        compiler_params=_params("parallel", "arbitrary", "arbitrary"),
        name="ffn_ln",
    )(x, w_gate, w_up, w_down, combine, g.reshape(1, d), b.reshape(1, d))


def _rel_bucket(dist):
    dist = jnp.maximum(dist, 0)
    max_exact = N_BUCKETS // 2
    df = jnp.maximum(dist, 1).astype(F32)
    large = max_exact + (jnp.log(df / max_exact) / math.log(MAX_DISTANCE / max_exact)
                         * (N_BUCKETS - max_exact)).astype(jnp.int32)
    large = jnp.clip(large, 0, N_BUCKETS - 1)
    return jnp.where(dist < max_exact, dist, large)


def _bias_by_distance(rel_bias, max_dist):
    return rel_bias[_rel_bucket(jnp.arange(max_dist + 1))].T.astype(F32)


def _half_masks(shape):
    lane = lax.broadcasted_iota(jnp.int32, shape, 1)
    return lane < HEAD_DIM, lane >= HEAD_DIM


def _half_ones(shape):
    lo, hi = _half_masks(shape)
    return jnp.where(lo, 1.0, 0.0).astype(BF16), jnp.where(hi, 1.0, 0.0).astype(BF16)


def _dil_prompt_body(q_ref, kp_ref, kc_ref, vp_ref, vc_ref, bias_ref, o_ref, lse_ref, s_ref, p_ref, mx_ref):
    n = N_KEYS
    first = pl.program_id(2) == 0
    pen = jnp.where(first, NEG, 0.0).astype(F32)
    lo, _ = _half_masks((n, LANES))
    ones = _half_ones((n, LANES))
    for pr in range(N_HEADS // 2):
        sl = slice(pr * LANES, (pr + 1) * LANES)
        q, kp, kc = q_ref[:, sl], kp_ref[:, sl], kc_ref[:, sl]
        for half in range(2):
            h = 2 * pr + half
            qm = q * ones[half]
            s_ref[h, :, :n] = _dot_nt(qm, kp) + bias_ref[h, :, :n] + pen
            s_ref[h, :, n:] = _dot_nt(qm, kc) + bias_ref[h, :, n:]
    for h in range(N_HEADS):
        s = s_ref[h]
        mx = jnp.max(s, axis=-1, keepdims=True)
        p_ref[h] = jnp.exp(s - mx).astype(BF16)
        mx_ref[h] = jnp.broadcast_to(mx, (n, LANES))
    for pr in range(N_HEADS // 2):
        sl = slice(pr * LANES, (pr + 1) * LANES)
        vp, vc = vp_ref[:, sl], vc_ref[:, sl]
        accs = []
        for half in range(2):
            h = 2 * pr + half
            one, other = ones[half], ones[1 - half]
            accs.append(_dot(p_ref[h, :, :n], vp * one + other) + _dot(p_ref[h, :, n:], vc * one + other))
        num = jnp.where(lo, accs[0], accs[1])
        den = jnp.where(lo, pltpu.roll(accs[0], HEAD_DIM, 1), pltpu.roll(accs[1], HEAD_DIM, 1))
        o_ref[:, sl] = num / den
        lse_ref[:, sl] = jnp.where(lo, mx_ref[2 * pr], mx_ref[2 * pr + 1]) + jnp.log(den)


def _linear_dil_body(x_ref, w_ref, o_ref, sc_ref, *, dil):
    acc = _dot(x_ref[...].astype(BF16), w_ref[...])
    rows = acc.shape[0] // dil
    for cb in range(acc.shape[1] // LANES):
        lanes = slice(cb * LANES, (cb + 1) * LANES)
        sc_ref[cb] = acc[:, lanes]
        for r in range(dil):
            o_ref[0, r, :, lanes] = sc_ref[cb, pl.ds(r, rows, stride=dil), :].astype(o_ref.dtype)


def linear_dil(x, w, dil, bsz, seq, tm=512, tn=1024):
    m, k = x.shape
    n = w.shape[1]
    tm = min(tm, seq)
    assert seq % tm == 0 and n % tn == 0 and tm % (16 * dil) == 0
    nbs = seq // tm
    return pl.pallas_call(
        functools.partial(_linear_dil_body, dil=dil),
        grid=(m // tm, n // tn),
        in_specs=[pl.BlockSpec((tm, k), lambda i, j: (i, 0)),
                  pl.BlockSpec((k, tn), lambda i, j: (0, j))],
        out_specs=pl.BlockSpec((1, dil, tm // dil, tn), lambda i, j: (i // nbs, 0, i % nbs, j)),
        out_shape=jax.ShapeDtypeStruct((bsz, dil, seq // dil, n), BF16),
        scratch_shapes=[pltpu.VMEM((tn // LANES, tm, LANES), F32)],
        compiler_params=_params("parallel", "parallel"),
        name="linear_dil",
    )(x, w)


def dilated_prompt_group(qkv, bias_tile, dil, bsz, seq):
    n = N_KEYS
    nb = seq // (dil * n)
    view = qkv.reshape(bsz * seq, 3 * D_MODEL)

    def cur(which):
        return lambda b, r, i: ((b * dil + r) * nb + i, which)

    def prev(which):
        return lambda b, r, i: ((b * dil + r) * nb + jnp.maximum(i - 1, 0), which)

    blk = (n, D_MODEL)
    o, lse = pl.pallas_call(
        _dil_prompt_body,
        grid=(bsz, dil, nb),
        in_specs=[pl.BlockSpec(blk, cur(0)),
                  pl.BlockSpec(blk, prev(1)), pl.BlockSpec(blk, cur(1)),
                  pl.BlockSpec(blk, prev(2)), pl.BlockSpec(blk, cur(2)),
                  pl.BlockSpec((N_HEADS, n, 2 * n), lambda b, r, i: (0, 0, 0))],
        out_specs=[pl.BlockSpec(blk, cur(0)), pl.BlockSpec(blk, cur(0))],
        out_shape=[jax.ShapeDtypeStruct((bsz * seq, D_MODEL), F32)] * 2,
        scratch_shapes=[pltpu.VMEM((N_HEADS, n, 2 * n), F32), pltpu.VMEM((N_HEADS, n, 2 * n), BF16),
                        pltpu.VMEM((N_HEADS, n, LANES), F32)],
        compiler_params=_params("parallel", "parallel", "arbitrary"),
        name="dilated_prompt_group",
    )(view, view, view, view, view, bias_tile)
    shape = (bsz, dil, seq // dil, D_MODEL)
    return o.reshape(shape), lse.reshape(shape)


def _merge_o_ln_body(o1, o2, o3, l1, l2, l3, w_ref, r_ref, g_ref, b_ref, out_ref, nat_ref):
    tm = out_ref.shape[0]
    ys = []
    for cb in range(D_MODEL // LANES):
        lanes = slice(cb * LANES, (cb + 1) * LANES)
        vals = []
        for j, src in enumerate((o1, o2, o3, l1, l2, l3)):
            dil = src.shape[1]
            if dil == 1:
                vals.append(src[0, 0, :, lanes])
                continue
            for r in range(dil):
                nat_ref[j, pl.ds(r, tm // dil, stride=dil), :] = src[0, r, :, lanes]
            vals.append(nat_ref[j])
        a, b, c = vals[3:]
        mx = jnp.maximum(jnp.maximum(a, b), c)
        ea, eb, ec = jnp.exp(a - mx), jnp.exp(b - mx), jnp.exp(c - mx)
        ys.append(((ea * vals[0] + eb * vals[1] + ec * vals[2]) / (ea + eb + ec)).astype(BF16))
    sub = _dot(jnp.concatenate(ys, axis=1), w_ref[...])
    out_ref[...] = _layer_norm(ALPHA * r_ref[...] + sub, g_ref[...], b_ref[...])


def merge_o_ln(outs, lses, w_o, resid, g, b, bsz, seq, tm=256):
    m, d = resid.shape
    tm = min(tm, seq)
    assert seq % tm == 0
    nbs = seq // tm
    row = pl.BlockSpec((tm, d), lambda i: (i, 0))
    vec = pl.BlockSpec((1, d), lambda i: (0, 0))

    def grp(a):
        dil = a.shape[1]
        assert tm % (8 * dil) == 0
        return pl.BlockSpec((1, dil, tm // dil, d), lambda i: (i // nbs, 0, i % nbs, 0))

    return pl.pallas_call(
        _merge_o_ln_body,
        grid=(m // tm,),
        in_specs=[grp(a) for a in (*outs, *lses)] + [pl.BlockSpec((d, d), lambda i: (0, 0)), row, vec, vec],
        out_specs=row,
        out_shape=jax.ShapeDtypeStruct((m, d), F32),
        scratch_shapes=[pltpu.VMEM((6, tm, LANES), F32)],
        compiler_params=_params("parallel"),
        name="merge_o_ln",
    )(*outs, *lses, w_o, resid, g.reshape(1, d), b.reshape(1, d))


def _dil_bias_tiles(bias_dist):
    n = N_KEYS
    period = 3 * n - 1
    tiles = []
    for _, dil in DIL_GROUPS:
        inside = bias_dist[:, 0:n * dil + 1:dil][:, ::-1]
        w = jnp.concatenate([inside, jnp.full((N_HEADS, period - (n + 1)), NEG, F32)], axis=1)
        skew = jnp.tile(w, (1, n))[:, :n * (period - 1)].reshape(N_HEADS, n, period - 1)
        tiles.append(skew[:, :, :2 * n])
    return tiles


SAMPLE_HEADS_PER_STEP = 4


def _bf16_round(x):
    return x.astype(BF16).astype(F32)


def _rows_to_cols(x):
    n = x.shape[1]
    r = lax.broadcasted_iota(jnp.int32, (n, n), 0)
    c = lax.broadcasted_iota(jnp.int32, (n, n), 1)
    eye = jnp.where(r == c, 1.0, 0.0).astype(BF16)
    hi, mid, lo = _split3(x)
    return _dot_nt(eye, hi) + _dot_nt(eye, mid) + _dot_nt(eye, lo)


def _dil_sample_body(q_ref, new_ref, b1_ref, b2_ref, b3_ref, a1_ref, a2_ref, a3_ref, a0_ref,
                     y_ref, o1_ref, o2_ref, o3_ref, s1_ref, s2_ref, s3_ref):
    hb = SAMPLE_HEADS_PER_STEP
    groups = ((b1_ref, a1_ref, o1_ref, s1_ref), (b2_ref, a2_ref, o2_ref, s2_ref), (b3_ref, a3_ref, o3_ref, s3_ref))
    q_rows = [_bf16_round(q_ref[0, 0, g] * ATTN_SCALE) for g in range(3)]
    q_cols = [_rows_to_cols(qr) for qr in q_rows]
    k_cols = [_rows_to_cols(new_ref[0, 0, g, 0]) for g in range(3)]
    v_cols = [_rows_to_cols(new_ref[0, 0, g, 1]) for g in range(3)]
    for g, (buf_ref, add_ref, out_ref, s_ref) in enumerate(groups):
        w = buf_ref.shape[-1]
        last = lax.broadcasted_iota(jnp.int32, (HEAD_DIM, w), 1) == w - 1
        for h in range(hb):
            k_buf, v_buf = buf_ref[0, 0, h], buf_ref[0, 1, h]
            out_ref[0, 0, h] = jnp.where(last, k_cols[g][:, h:h + 1], pltpu.roll(k_buf, w - 1, 1))
            out_ref[0, 1, h] = jnp.where(last, v_cols[g][:, h:h + 1], pltpu.roll(v_buf, w - 1, 1))
            s_ref[h:h + 1, :] = jnp.sum(_bf16_round(k_buf) * q_cols[g][:, h:h + 1], axis=0, keepdims=True)
    p_news, dens, lses = [], [], []
    for g, (buf_ref, add_ref, out_ref, s_ref) in enumerate(groups):
        s = s_ref[...] + add_ref[0]
        s_new = (jnp.sum(q_rows[g] * _bf16_round(new_ref[0, 0, g, 0]), axis=1, keepdims=True)
                 + a0_ref[0][:, :1])
        mx = jnp.maximum(jnp.max(s, axis=1, keepdims=True), s_new)
        p = jnp.exp(s - mx)
        p_new = jnp.exp(s_new - mx)
        den = jnp.sum(p, axis=1, keepdims=True) + p_new
        s_ref[...] = _bf16_round(p)
        p_news.append(_bf16_round(p_new))
        dens.append(den)
        lses.append(mx + jnp.log(den))
    mx = jnp.maximum(jnp.maximum(lses[0], lses[1]), lses[2])
    ws = [jnp.exp(l - mx) for l in lses]
    wsum = ws[0] + ws[1] + ws[2]
    coef = [ws[g] / (wsum * dens[g]) for g in range(3)]
    for h in range(hb):
        y = jnp.zeros((HEAD_DIM, 1), F32)
        for g, (buf_ref, add_ref, out_ref, s_ref) in enumerate(groups):
            acc = (jnp.sum(_bf16_round(buf_ref[0, 1, h]) * s_ref[h:h + 1, :], axis=1, keepdims=True)
                   + p_news[g][h:h + 1, :] * _bf16_round(v_cols[g][:, h:h + 1]))
            y = y + coef[g][h:h + 1, :] * acc
        y_ref[0, h] = y


def dilated_sample(qkv, bufs, bias_dist):
    bsz = qkv.shape[0]
    hb = SAMPLE_HEADS_PER_STEP
    n_hg = N_HEADS // hb
    parts = qkv.reshape(bsz, 3, 3, n_hg, hb, HEAD_DIM)
    q_rows = parts[:, :, 0].transpose(0, 2, 1, 3, 4)
    new_rows = parts[:, :, 1:3].transpose(0, 3, 1, 2, 4, 5)
    bufs_t, adds = [], []
    for buf, (window, dil) in zip(bufs, DIL_GROUPS):
        assert buf.shape[1] == window
        bufs_t.append(buf.transpose(0, 2, 3, 4, 1))
        pos = np.arange(window)
        add = bias_dist[:, 1:window + 1][:, ::-1]
        adds.append(jnp.where(jnp.asarray(pos % dil == 0)[None], add, NEG).reshape(n_hg, hb, window))
    add0 = jnp.broadcast_to(bias_dist[:, :1], (N_HEADS, LANES)).reshape(n_hg, hb, LANES)

    def buf_spec(window):
        return pl.BlockSpec((1, 2, hb, HEAD_DIM, window), lambda b, j: (b, 0, j, 0, 0))

    windows = [w for w, _ in DIL_GROUPS]
    res = pl.pallas_call(
        _dil_sample_body,
        grid=(bsz, n_hg),
        in_specs=[pl.BlockSpec((1, 1, 3, hb, HEAD_DIM), lambda b, j: (b, j, 0, 0, 0)),
                  pl.BlockSpec((1, 1, 3, 2, hb, HEAD_DIM), lambda b, j: (b, j, 0, 0, 0, 0))]
                 + [buf_spec(w) for w in windows]
                 + [pl.BlockSpec((1, hb, w), lambda b, j: (j, 0, 0)) for w in windows]
                 + [pl.BlockSpec((1, hb, LANES), lambda b, j: (j, 0, 0))],
        out_specs=[pl.BlockSpec((1, hb, HEAD_DIM, 1), lambda b, j: (b, j, 0, 0))] + [buf_spec(w) for w in windows],
        out_shape=[jax.ShapeDtypeStruct((bsz, N_HEADS, HEAD_DIM, 1), F32)]
                  + [jax.ShapeDtypeStruct(bt.shape, F32) for bt in bufs_t],
        scratch_shapes=[pltpu.VMEM((hb, w), F32) for w in windows],
        compiler_params=_params("parallel", "parallel"),
        name="dilated_sample",
    )(q_rows, new_rows, *bufs_t, *adds, add0)
    new_bufs = tuple(o.transpose(0, 4, 1, 2, 3) for o in res[1:])
    return res[0].reshape(bsz, D_MODEL), new_bufs


def _diff_lambda(lp, lam_init):
    a = jnp.sum(lp[0:1] * lp[1:2], axis=-1, keepdims=True)
    b = jnp.sum(lp[2:3] * lp[3:4], axis=-1, keepdims=True)
    return jnp.exp(a) - jnp.exp(b) + lam_init


def _diff_prompt_body(q_ref, k_ref, v_ref, rb_ref, lam_ref, g_ref, o_ref, bias_ref, m_ref, acc_ref,
                      *, t, n_near, lam_init):
    qi = pl.program_id(2)

    @pl.when(qi == 0)
    def _():
        for slot in range(2):
            for c in range(n_near):
                win = jnp.broadcast_to(rb_ref[0, slot:slot + 1, c * t:(c + 2) * t], (t, 2 * t))
                bias_ref[slot, c] = pltpu.roll(win, t + 1, 1, stride=1, stride_axis=0)[:, :t]

    q = q_ref[...]
    lo, hi = _half_ones(q.shape)
    qs = (q * lo, q * hi)
    m_ref[...] = jnp.full(m_ref.shape, NEG, F32)
    acc_ref[...] = jnp.zeros(acc_ref.shape, F32)
    ones = jnp.ones((t, LANES), BF16)

    def chunk(kj, c):
        start = pl.multiple_of(kj * t, t)
        k = k_ref[pl.ds(start, t), :]
        rhs = jnp.concatenate([v_ref[pl.ds(start, t), :], ones], axis=1)
        ss = [_dot_nt(qs[slot], k) for slot in range(2)]
        if c is not None:
            ss = [ss[slot] + bias_ref[slot, c] for slot in range(2)]
        m_prevs = [m_ref[slot] for slot in range(2)]
        m_news = [jnp.maximum(m_prevs[slot], jnp.max(ss[slot], axis=-1, keepdims=True)) for slot in range(2)]
        ps = [jnp.exp(ss[slot] - jnp.tile(m_news[slot], (1, t // LANES))).astype(BF16) for slot in range(2)]
        pvs = [_dot(ps[slot], rhs) for slot in range(2)]
        for slot in range(2):
            alpha = jnp.exp(m_prevs[slot] - m_news[slot])
            acc_ref[slot] = jnp.tile(alpha, (1, 2)) * acc_ref[slot] + pvs[slot]
            m_ref[slot] = m_news[slot]

    def far_body(kj, carry):
        chunk(kj, None)
        return carry

    lax.fori_loop(0, jnp.maximum(qi - (n_near - 1), 0), far_body, 0)
    for c in range(n_near):
        kj = qi - (n_near - 1) + c

        @pl.when(kj >= 0)
        def _(kj=kj, c=c):
            chunk(kj, c)

    lam = _diff_lambda(lam_ref[...], lam_init)
    o = (acc_ref[0, :, :LANES] / acc_ref[0, :, LANES:]
         - lam * (acc_ref[1, :, :LANES] / acc_ref[1, :, LANES:]))
    o = o * lax.rsqrt(jnp.mean(o * o, axis=-1, keepdims=True) + RMS_EPS)
    o_ref[...] = (o * g_ref[...] * (1.0 - lam_init)).astype(o_ref.dtype)


def diff_prompt(q, kv, rev_bias, lam_params, subln_g, lam_init, bsz, seq, t=512):
    nq = seq // t
    n_near = rev_bias.shape[-1] // t - 1
    body = functools.partial(_diff_prompt_body, t=t, n_near=n_near, lam_init=lam_init)
    return pl.pallas_call(
        body,
        grid=(bsz, H_DIFF, nq),
        in_specs=[pl.BlockSpec((t, LANES), lambda b, h, i: (b * nq + i, h)),
                  pl.BlockSpec((seq, LANES), lambda b, h, i: (b, h)),
                  pl.BlockSpec((seq, LANES), lambda b, h, i: (b, H_DIFF + h)),
                  pl.BlockSpec((1, 2, (n_near + 1) * t), lambda b, h, i: (h, 0, 0)),
                  pl.BlockSpec((4, HEAD_DIM), lambda b, h, i: (0, 0)),
                  pl.BlockSpec((1, LANES), lambda b, h, i: (0, 0))],
        out_specs=pl.BlockSpec((t, LANES), lambda b, h, i: (b * nq + i, h)),
        out_shape=jax.ShapeDtypeStruct((bsz * seq, D_MODEL), BF16),
        scratch_shapes=[pltpu.VMEM((2, n_near, t, t), F32), pltpu.VMEM((2, t, LANES), F32),
                        pltpu.VMEM((2, t, 2 * LANES), F32)],
        compiler_params=_params("parallel", "parallel", "arbitrary"),
        name="diff_prompt",
    )(q, kv, kv, rev_bias, lam_params, subln_g.reshape(1, LANES))


def _diff_rev_bias(bias_dist, t):
    n_near = -(-(FAR_DIST + t - 1) // t)
    d_max = n_near * t - 1
    assert bias_dist.shape[1] > d_max
    rel = bias_dist[:, :d_max + 1] - bias_dist[:, -1:]
    rev = jnp.concatenate([rel[:, ::-1], jnp.full((N_HEADS, t), NEG, F32)], axis=1)
    return rev.reshape(H_DIFF, 2, (n_near + 1) * t)


PAGES_PER_STEP = 8


def _diff_sample_body(pt_ref, q_ref, new_ref, *refs, lam_init):
    page_refs = refs[:PAGES_PER_STEP]
    add_ref, add0_ref, lam_ref, g_ref, o_ref, m_ref, l_ref, acc_ref = refs[PAGES_PER_STEP:]
    c = pl.program_id(1)
    rows_per_page = PAGE_SIZE * H_DIFF
    q8 = q_ref[0] * ATTN_SCALE
    lo, hi = _half_masks((H_DIFF, LANES))
    qs = jnp.concatenate([jnp.where(lo, q8, 0.0), jnp.where(hi, q8, 0.0)], axis=0).astype(BF16)

    @pl.when(c == 0)
    def _():
        m_ref[...] = jnp.full(m_ref.shape, NEG, F32)
        l_ref[...] = jnp.zeros(l_ref.shape, F32)
        acc_ref[...] = jnp.zeros(acc_ref.shape, F32)

    s = jnp.concatenate([_dot_nt(qs, pr[0, :, 0].reshape(rows_per_page, LANES).astype(BF16)) for pr in page_refs],
                        axis=-1)
    s = s + add_ref[...]
    m_prev = m_ref[...]
    m_new = jnp.maximum(m_prev, jnp.max(s, axis=-1, keepdims=True))
    alpha = jnp.exp(m_prev - m_new)
    p = jnp.exp(s - m_new[:, :1]).astype(BF16)
    l_ref[...] = alpha * l_ref[...] + jnp.sum(p.astype(F32), axis=-1, keepdims=True)
    pv = jnp.zeros((2 * H_DIFF, LANES), F32)
    for i, pr in enumerate(page_refs):
        v2 = pr[0, :, 1].reshape(rows_per_page, LANES).astype(BF16)
        pv = pv + _dot(p[:, i * rows_per_page:(i + 1) * rows_per_page], v2)
    acc_ref[...] = alpha * acc_ref[...] + pv
    m_ref[...] = m_new

    @pl.when(c == pl.num_programs(1) - 1)
    def _():
        k_new, v_new = _bf16_round(new_ref[0, 0]), _bf16_round(new_ref[0, 1])
        k16 = jnp.concatenate([k_new, k_new], axis=0)
        v16 = jnp.concatenate([v_new, v_new], axis=0)
        s_new = jnp.sum(qs.astype(F32) * k16, axis=-1, keepdims=True) + add0_ref[:, :1]
        m_prev = m_ref[...][:, :1]
        m_fin = jnp.maximum(m_prev, s_new)
        a_fin = jnp.exp(m_prev - m_fin)
        p_new = jnp.exp(s_new - m_fin)
        den = a_fin * l_ref[...][:, :1] + p_new
        res = (a_fin * acc_ref[...] + _bf16_round(p_new) * v16) / den
        lam = _diff_lambda(lam_ref[...], lam_init)
        o = res[:H_DIFF] - lam * res[H_DIFF:]
        o = o * lax.rsqrt(jnp.mean(o * o, axis=-1, keepdims=True) + RMS_EPS)
        o_ref[0] = o * g_ref[...] * (1.0 - lam_init)


def diff_sample(qkv, cache, page_table, bias_dist, lam_params, subln_g, lam_init):
    bsz, n_pages = page_table.shape
    assert n_pages % PAGES_PER_STEP == 0
    n_past = n_pages * PAGE_SIZE
    rows = 2 * H_DIFF
    order = np.array([2 * (r % H_DIFF) + r // H_DIFF for r in range(rows)])
    bias_rows = bias_dist[order]
    past = bias_rows[:, 1:n_past + 1][:, ::-1]
    own = (np.arange(rows)[:, None] % H_DIFF) == np.arange(H_DIFF)[None, :]
    add = jnp.where(jnp.asarray(own)[:, None, :], past[:, :, None], NEG).reshape(rows, n_past * H_DIFF)
    add0 = jnp.broadcast_to(bias_rows[:, :1], (rows, LANES))
    step_lanes = PAGES_PER_STEP * PAGE_SIZE * H_DIFF

    def page_spec(i):
        return pl.BlockSpec((1, PAGE_SIZE, 2, H_DIFF, LANES),
                            lambda b, c, pt: (pt[b * n_pages + c * PAGES_PER_STEP + i], 0, 0, 0, 0))

    grid_spec = pltpu.PrefetchScalarGridSpec(
        num_scalar_prefetch=1,
        grid=(bsz, n_pages // PAGES_PER_STEP),
        in_specs=[pl.BlockSpec((1, H_DIFF, LANES), lambda b, c, pt: (b, 0, 0)),
                  pl.BlockSpec((1, 2, H_DIFF, LANES), lambda b, c, pt: (b, 0, 0, 0))]
                 + [page_spec(i) for i in range(PAGES_PER_STEP)]
                 + [pl.BlockSpec((rows, step_lanes), lambda b, c, pt: (0, c)),
                    pl.BlockSpec((rows, LANES), lambda b, c, pt: (0, 0)),
                    pl.BlockSpec((4, HEAD_DIM), lambda b, c, pt: (0, 0)),
                    pl.BlockSpec((1, LANES), lambda b, c, pt: (0, 0))],
        out_specs=pl.BlockSpec((1, H_DIFF, LANES), lambda b, c, pt: (b, 0, 0)),
        scratch_shapes=[pltpu.VMEM((rows, LANES), F32), pltpu.VMEM((rows, LANES), F32),
                        pltpu.VMEM((rows, LANES), F32)],
    )
    out = pl.pallas_call(
        functools.partial(_diff_sample_body, lam_init=lam_init),
        grid_spec=grid_spec,
        out_shape=jax.ShapeDtypeStruct((bsz, H_DIFF, LANES), F32),
        compiler_params=_params("parallel", "arbitrary"),
        name="diff_sample",
    )(page_table.reshape(-1), qkv[:, :D_MODEL].reshape(bsz, H_DIFF, LANES),
      qkv[:, D_MODEL:].reshape(bsz, 2, H_DIFF, LANES), *([cache] * PAGES_PER_STEP),
      add, add0, lam_params, subln_g.reshape(1, LANES))
    return out.reshape(bsz, D_MODEL)


def _split3(x):
    hi = x.astype(BF16)
    r1 = x - hi.astype(F32)
    mid = r1.astype(BF16)
    lo = (r1 - mid.astype(F32)).astype(BF16)
    return hi, mid, lo


def _tri_cumsum(x, tri):
    hi, mid, lo = _split3(x)
    return _dot(tri, hi) + _dot(tri, mid) + _dot(tri, lo)


def _lower_tri(n):
    r = lax.broadcasted_iota(jnp.int32, (n, n), 0)
    c = lax.broadcasted_iota(jnp.int32, (n, n), 1)
    return jnp.where(r >= c, 1.0, 0.0).astype(BF16)


def _cumsum_body(x_ref, o_ref, carry_ref, *, chunk):
    tri = _lower_tri(chunk)
    carry_ref[...] = jnp.zeros(carry_ref.shape, F32)

    def body(i, c):
        start = pl.multiple_of(i * chunk, chunk)
        cum = _tri_cumsum(x_ref[0, pl.ds(start, chunk), :], tri) + carry_ref[...]
        o_ref[0, pl.ds(start, chunk), :] = cum
        carry_ref[...] = cum[chunk - 1:chunk, :]
        return c

    lax.fori_loop(0, x_ref.shape[1] // chunk, body, 0)


def seq_cumsum(x, chunk=128):
    bsz, seq, c = x.shape
    assert seq % chunk == 0
    return pl.pallas_call(
        functools.partial(_cumsum_body, chunk=chunk),
        grid=(bsz,),
        in_specs=[pl.BlockSpec((1, seq, c), lambda b: (b, 0, 0))],
        out_specs=pl.BlockSpec((1, seq, c), lambda b: (b, 0, 0)),
        out_shape=jax.ShapeDtypeStruct((bsz, seq, c), F32),
        scratch_shapes=[pltpu.VMEM((1, c), F32)],
        compiler_params=_params("parallel"),
        name="seq_cumsum",
    )(x)


def _fox_prompt_body(q_ref, k_ref, v_ref, ccol_ref, crow_ref, o_ref, m_ref, acc_ref, *, t):
    qi = pl.program_id(2)
    q = q_ref[...]
    ones = _half_ones(q.shape)
    qs = [q * one for one in ones]
    m_ref[...] = jnp.full(m_ref.shape, NEG, F32)
    acc_ref[...] = jnp.zeros(acc_ref.shape, F32)
    cq = ccol_ref[0, 0]
    cref = [cq[0:1, s:s + 1] for s in range(2)]
    a_rep = [jnp.broadcast_to(cq[:, s:s + 1] - cref[s], (t, LANES)) for s in range(2)]

    def chunk(kj, diag):
        start = pl.multiple_of(kj * t, t)
        k = k_ref[pl.ds(start, t), :]
        v = v_ref[pl.ds(start, t), :]
        crow = crow_ref[0, 0, :, pl.ds(start, t)]
        ss = [_dot_nt(qs[slot], k) + jnp.tile(a_rep[slot], (1, t // LANES))
              - (crow[slot:slot + 1, :] - cref[slot]) for slot in range(2)]
        if diag:
            r = lax.broadcasted_iota(jnp.int32, (t, t), 0)
            c = lax.broadcasted_iota(jnp.int32, (t, t), 1)
            ss = [jnp.where(r >= c, s, NEG) for s in ss]
        m_prevs = [m_ref[slot] for slot in range(2)]
        m_news = [jnp.maximum(m_prevs[slot], jnp.max(ss[slot], axis=-1, keepdims=True)) for slot in range(2)]
        ps = [jnp.exp(ss[slot] - jnp.tile(m_news[slot], (1, t // LANES))).astype(BF16) for slot in range(2)]
        pvs = [_dot(ps[slot], v * ones[slot] + ones[1 - slot]) for slot in range(2)]
        for slot in range(2):
            acc_ref[slot] = jnp.exp(m_prevs[slot] - m_news[slot]) * acc_ref[slot] + pvs[slot]
            m_ref[slot] = m_news[slot]

    def body(kj, carry):
        chunk(kj, False)
        return carry

    lax.fori_loop(0, qi, body, 0)
    chunk(qi, True)
    lo, _ = _half_masks((t, LANES))
    num = jnp.where(lo, acc_ref[0], acc_ref[1])
    den = jnp.where(lo, pltpu.roll(acc_ref[0], HEAD_DIM, 1), pltpu.roll(acc_ref[1], HEAD_DIM, 1))
    o_ref[...] = (num / den).astype(o_ref.dtype)


def fox_prompt(q, kv, cum, bsz, seq, t=512):
    nq = seq // t
    n_pair = N_HEADS // 2
    ccol = cum.reshape(bsz, seq, n_pair, 2).transpose(0, 2, 1, 3)
    crow = cum.reshape(bsz, seq, n_pair, 2).transpose(0, 2, 3, 1)
    return pl.pallas_call(
        functools.partial(_fox_prompt_body, t=t),
        grid=(bsz, n_pair, nq),
        in_specs=[pl.BlockSpec((t, LANES), lambda b, h, i: (b * nq + i, h)),
                  pl.BlockSpec((seq, LANES), lambda b, h, i: (b, h)),
                  pl.BlockSpec((seq, LANES), lambda b, h, i: (b, n_pair + h)),
                  pl.BlockSpec((1, 1, t, 2), lambda b, h, i: (b, h, i, 0)),
                  pl.BlockSpec((1, 1, 2, seq), lambda b, h, i: (b, h, 0, 0))],
        out_specs=pl.BlockSpec((t, LANES), lambda b, h, i: (b * nq + i, h)),
        out_shape=jax.ShapeDtypeStruct((bsz * seq, D_MODEL), BF16),
        scratch_shapes=[pltpu.VMEM((2, t, LANES), F32), pltpu.VMEM((2, t, LANES), F32)],
        compiler_params=_params("parallel", "parallel", "arbitrary"),
        name="fox_prompt",
    )(q, kv, kv, ccol, crow)


def _fox_sample_body(pt_ref, q_ref, new_ref, lfn_ref, *refs, n_pages):
    kv_refs = refs[:PAGES_PER_STEP]
    lf_refs = refs[PAGES_PER_STEP:PAGES_PER_STEP + n_pages]
    y_ref, dec_ref, s_ref, m_ref, l_ref, acc_ref = refs[PAGES_PER_STEP + n_pages:]
    c = pl.program_id(1)
    width = PAGES_PER_STEP * PAGE_SIZE

    @pl.when(c == 0)
    def _():
        r = lax.broadcasted_iota(jnp.int32, (PAGE_SIZE, PAGE_SIZE), 0)
        col = lax.broadcasted_iota(jnp.int32, (PAGE_SIZE, PAGE_SIZE), 1)
        upper = jnp.where(r > col, 1.0, 0.0).astype(BF16)
        carry = lfn_ref[0]
        for i in reversed(range(n_pages)):
            x = lf_refs[i][0]
            hi, mid, lo = _split3(x)
            dec_ref[:, i * PAGE_SIZE:(i + 1) * PAGE_SIZE] = _dot(hi, upper) + _dot(mid, upper) + _dot(lo, upper) + carry
            carry = carry + jnp.sum(x, axis=1, keepdims=True)
        m_ref[...] = jnp.full(m_ref.shape, NEG, F32)
        l_ref[...] = jnp.zeros(l_ref.shape, F32)
        acc_ref[...] = jnp.zeros(acc_ref.shape, F32)

    q_cols = _bf16_round(_rows_to_cols(q_ref[0]) * ATTN_SCALE)
    qs = [q_cols[:, h:h + 1] for h in range(N_HEADS)]
    for i in range(PAGES_PER_STEP):
        for h in range(N_HEADS):
            s_ref[h:h + 1, i * PAGE_SIZE:(i + 1) * PAGE_SIZE] = jnp.sum(
                _bf16_round(kv_refs[i][0, 0, h]) * qs[h], axis=0, keepdims=True)
    s = s_ref[...] + dec_ref[:, pl.ds(pl.multiple_of(c * width, width), width)]
    m_prev = m_ref[...]
    m_new = jnp.maximum(m_prev, jnp.max(s, axis=1, keepdims=True))
    alpha = jnp.exp(m_prev - m_new)
    p = jnp.exp(s - m_new[:, :1])
    l_ref[...] = alpha * l_ref[...] + jnp.sum(p, axis=1, keepdims=True)
    m_ref[...] = m_new
    s_ref[...] = _bf16_round(p)
    for h in range(N_HEADS):
        acc = alpha[h:h + 1, :] * acc_ref[h]
        for i in range(PAGES_PER_STEP):
            acc = acc + _bf16_round(kv_refs[i][0, 1, h]) * s_ref[h:h + 1, i * PAGE_SIZE:(i + 1) * PAGE_SIZE]
        acc_ref[h] = acc

    @pl.when(c == pl.num_programs(1) - 1)
    def _():
        k_cols = _bf16_round(_rows_to_cols(new_ref[0, 0]))
        v_cols = _bf16_round(_rows_to_cols(new_ref[0, 1]))
        for h in range(N_HEADS):
            k_new, v_new = k_cols[:, h:h + 1], v_cols[:, h:h + 1]
            s_new = jnp.sum(qs[h] * k_new, axis=0, keepdims=True)
            m_h, l_h = m_ref[h:h + 1, :1], l_ref[h:h + 1, :1]
            m_fin = jnp.maximum(m_h, s_new)
            a_fin = jnp.exp(m_h - m_fin)
            p_new = jnp.exp(s_new - m_fin)
            num = a_fin * jnp.sum(acc_ref[h], axis=1, keepdims=True) + _bf16_round(p_new) * v_new
            y_ref[0, h] = num / (a_fin * l_h + p_new)


def fox_sample(qkv, cache_kv, cache_logf, page_table, logf_new):
    bsz, n_pages = page_table.shape
    assert n_pages % PAGES_PER_STEP == 0
    kv_t = cache_kv.transpose(0, 2, 3, 4, 1)
    lf_t = cache_logf.transpose(0, 2, 1)
    q_rows = qkv[:, :D_MODEL].reshape(bsz, N_HEADS, HEAD_DIM)
    new_rows = qkv[:, D_MODEL:].reshape(bsz, 2, N_HEADS, HEAD_DIM)

    def kv_spec(i):
        return pl.BlockSpec((1, 2, N_HEADS, HEAD_DIM, PAGE_SIZE),
                            lambda b, c, pt: (pt[b * n_pages + c * PAGES_PER_STEP + i], 0, 0, 0, 0))

    def lf_spec(i):
        return pl.BlockSpec((1, N_HEADS, PAGE_SIZE), lambda b, c, pt: (pt[b * n_pages + i], 0, 0))

    past = n_pages * PAGE_SIZE
    grid_spec = pltpu.PrefetchScalarGridSpec(
        num_scalar_prefetch=1,
        grid=(bsz, n_pages // PAGES_PER_STEP),
        in_specs=[pl.BlockSpec((1, N_HEADS, HEAD_DIM), lambda b, c, pt: (b, 0, 0)),
                  pl.BlockSpec((1, 2, N_HEADS, HEAD_DIM), lambda b, c, pt: (b, 0, 0, 0)),
                  pl.BlockSpec((1, N_HEADS, 1), lambda b, c, pt: (b, 0, 0))]
                 + [kv_spec(i) for i in range(PAGES_PER_STEP)] + [lf_spec(i) for i in range(n_pages)],
        out_specs=pl.BlockSpec((1, N_HEADS, HEAD_DIM, 1), lambda b, c, pt: (b, 0, 0, 0)),
        scratch_shapes=[pltpu.VMEM((N_HEADS, past), F32),
                        pltpu.VMEM((N_HEADS, PAGES_PER_STEP * PAGE_SIZE), F32),
                        pltpu.VMEM((N_HEADS, LANES), F32), pltpu.VMEM((N_HEADS, LANES), F32),
                        pltpu.VMEM((N_HEADS, HEAD_DIM, PAGE_SIZE), F32)],
    )
    out = pl.pallas_call(
        functools.partial(_fox_sample_body, n_pages=n_pages),
        grid_spec=grid_spec,
        out_shape=jax.ShapeDtypeStruct((bsz, N_HEADS, HEAD_DIM, 1), F32),
        compiler_params=_params("parallel", "arbitrary"),
        name="fox_sample",
    )(page_table.reshape(-1), q_rows, new_rows, logf_new.reshape(bsz, N_HEADS, 1),
      *([kv_t] * PAGES_PER_STEP), *([lf_t] * n_pages))
    return out.reshape(bsz, D_MODEL)


def _pad_cols(w, n):
    return jnp.pad(w, [(0, 0)] * (w.ndim - 1) + [(0, n - w.shape[-1])])


def _split_gu(w_gu):
    gate, up = w_gu[..., :D_FF], w_gu[..., D_FF:]
    return _pad_cols(gate, D_FF_PAD).astype(BF16), _pad_cols(up, D_FF_PAD).astype(BF16)


def _pad_down(w_down):
    pad = [(0, 0)] * (w_down.ndim - 2) + [(0, D_FF_PAD - D_FF), (0, 0)]
    return jnp.pad(w_down, pad).astype(BF16)


def _mixer_a(hp, hs, bufs, w_qkv, w_o, ln_g, ln_b, bias_dist, bsz, seq):
    w = w_qkv.astype(BF16)
    col = np.arange(w.shape[1]) % (3 * D_MODEL) < D_MODEL
    wq_scaled = jnp.where(jnp.asarray(col)[None, :], w * ATTN_SCALE, w)
    (qkv_s,) = linear(hs, w, [F32])

    tail = min(DIL_GROUPS[-1][0], seq)
    kv_cols = np.concatenate([np.arange(g * 3 * D_MODEL + D_MODEL, (g + 1) * 3 * D_MODEL) for g in range(3)])
    x_tail = hp.reshape(bsz, seq, D_MODEL)[:, seq - tail:].reshape(bsz * tail, D_MODEL)
    (kv_tail,) = linear(x_tail, w[:, kv_cols], [F32])
    kv_tail = kv_tail.reshape(bsz, tail, 3, 2, N_HEADS, HEAD_DIM)
    bufs_p = tuple(kv_tail[:, tail - min(wd, seq):, g] for g, (wd, _) in enumerate(DIL_GROUPS))

    tiles = _dil_bias_tiles(bias_dist)
    outs, lses = [], []
    for g, (_, dil) in enumerate(DIL_GROUPS):
        qkv_g = linear_dil(hp, wq_scaled[:, g * 3 * D_MODEL:(g + 1) * 3 * D_MODEL], dil, bsz, seq)
        o, l = dilated_prompt_group(qkv_g, tiles[g], dil, bsz, seq)
        outs.append(o)
        lses.append(l)
    hp = merge_o_ln(outs, lses, w_o, hp, ln_g, ln_b, bsz, seq)

    ys, bufs_s = dilated_sample(qkv_s, bufs, bias_dist)
    hs = linear_ln(ys, w_o, hs, ln_g, ln_b)
    return hp, hs, bufs_p, bufs_s


def _mixer_b(hp, hs, cache_kv, page_table, w_qkv, lam_params, subln_g, w_o, ln_g, ln_b, bias_dist,
             layer_idx, bsz, seq):
    w = w_qkv.astype(BF16)
    lam_init = 0.8 - 0.6 * math.exp(-0.3 * layer_idx)
    (q_p,) = linear(hp, w[:, :D_MODEL], [BF16], scale=ATTN_SCALE)
    kv_p, kv_pb = linear(hp, w[:, D_MODEL:], [F32, BF16])
    (qkv_s,) = linear(hs, w, [F32])
    t = min(512, seq)
    o = diff_prompt(q_p, kv_pb, _diff_rev_bias(bias_dist, t), lam_params, subln_g, lam_init, bsz, seq, t)
    hp = linear_ln(o, w_o, hp, ln_g, ln_b)

    dec = hs.shape[0]
    ys = diff_sample(qkv_s, cache_kv, page_table, bias_dist, lam_params, subln_g, lam_init)
    hs = linear_ln(ys, w_o, hs, ln_g, ln_b)
    st_p = kv_p.reshape(bsz, seq, 2, H_DIFF, 2 * HEAD_DIM)
    st_s = qkv_s[:, D_MODEL:].reshape(dec, 1, 2, H_DIFF, 2 * HEAD_DIM)
    return hp, hs, st_p, st_s


def _mixer_c(hp, hs, cache_kv, cache_logf, page_table, w_qkv, w_f, b_f, w_o, ln_g, ln_b, bsz, seq):
    w = w_qkv.astype(BF16)
    wf = _pad_cols(w_f, LANES).astype(BF16)
    bf = _pad_cols(b_f.reshape(1, N_HEADS), LANES).astype(F32)
    (q_p,) = linear(hp, w[:, :D_MODEL], [BF16], scale=ATTN_SCALE)
    kv_p, kv_pb = linear(hp, w[:, D_MODEL:], [F32, BF16])
    (qkv_s,) = linear(hs, w, [F32])
    logf_p = forget_log(hp, wf, bf)[:, :N_HEADS].reshape(bsz, seq, N_HEADS)
    logf_s = forget_log(hs, wf, bf)[:, :N_HEADS]
    cum = seq_cumsum(logf_p)
    o = fox_prompt(q_p, kv_pb, cum, bsz, seq)
    hp = linear_ln(o, w_o, hp, ln_g, ln_b)

    dec = hs.shape[0]
    ys = fox_sample(qkv_s, cache_kv, cache_logf, page_table, logf_s)
    hs = linear_ln(ys, w_o, hs, ln_g, ln_b)
    st_p = (kv_p.reshape(bsz, seq, 2, N_HEADS, HEAD_DIM), logf_p)
    st_s = (qkv_s[:, D_MODEL:].reshape(dec, 1, 2, N_HEADS, HEAD_DIM), logf_s.reshape(dec, 1, N_HEADS))
    return hp, hs, st_p, st_s


def _dense_ffn(h, w_gu, w_down, ln_g, ln_b):
    gate, up = _split_gu(w_gu)
    return ffn_ln(h, gate[None], up[None], _pad_down(w_down)[None], None, ln_g, ln_b)


def _moe_ffn(h, w_router, w_gu, w_down, ln_g, ln_b):
    combine = moe_router(h, _pad_cols(w_router, LANES).astype(BF16))
    gate, up = _split_gu(w_gu)
    return ffn_ln(h, gate, up, _pad_down(w_down), combine, ln_g, ln_b)


def kernel(x_prompt, x_sample, cache_l0_kv_w128, cache_l0_kv_w512, cache_l0_kv_w2048, cache_l1_kv, cache_l2_kv, cache_l2_logf, cache_l3_kv_w128, cache_l3_kv_w512, cache_l3_kv_w2048, page_table, p_prompt, p_sample, rel_bias, ln_g, ln_b, ple_w_in, ple_w_gate, ple_b_gate, a_w_qkv, a_w_o, b_w_qkv, b_lambda, b_subln_g, b_w_o, c_w_qkv, c_w_f, c_b_f, c_w_o, ffn_w_gu, ffn_w_down, moe_router, moe_w_gu, moe_w_down):
    bsz, seq, d = x_prompt.shape
    dec = x_sample.shape[0]
    assert d == D_MODEL and x_sample.shape[1] == 1 and seq % DIL_GROUPS[-1][0] == 0
    caches = ((cache_l0_kv_w128, cache_l0_kv_w512, cache_l0_kv_w2048), (cache_l1_kv,),
              (cache_l2_kv, cache_l2_logf), (cache_l3_kv_w128, cache_l3_kv_w512, cache_l3_kv_w2048))
    bias_dist = _bias_by_distance(rel_bias, max(MAX_DISTANCE, page_table.shape[1] * PAGE_SIZE))
    hp = x_prompt.reshape(bsz * seq, d)
    hs = x_sample.reshape(dec, d)
    states = []
    for i in range(DEPTH):
        kind, j = i % 3, i // 3
        g0, b0, g1, b1 = ln_g[i, 0], ln_b[i, 0], ln_g[i, 1], ln_b[i, 1]
        if kind == 0:
            hp, hs, st_p, st_s = _mixer_a(hp, hs, caches[i], a_w_qkv[j], a_w_o[j].astype(BF16), g0, b0,
                                          bias_dist, bsz, seq)
        elif kind == 1:
            hp, hs, st_p, st_s = _mixer_b(hp, hs, caches[i][0], page_table, b_w_qkv[j], b_lambda[j],
                                          b_subln_g[j], b_w_o[j].astype(BF16), g0, b0, bias_dist, i, bsz, seq)
            st_p, st_s = (st_p,), (st_s,)
        else:
            hp, hs, st_p, st_s = _mixer_c(hp, hs, caches[i][0], caches[i][1], page_table, c_w_qkv[j],
                                          c_w_f[j], c_b_f[j], c_w_o[j].astype(BF16), g0, b0, bsz, seq)
        states.append((st_p, st_s))
        f = i // 2
        if i % 2 == 0:
            hp = _dense_ffn(hp, ffn_w_gu[f], ffn_w_down[f], g1, b1)
            hs = _dense_ffn(hs, ffn_w_gu[f], ffn_w_down[f], g1, b1)
        else:
            hp = _moe_ffn(hp, moe_router[f], moe_w_gu[f], moe_w_down[f], g1, b1)
            hs = _moe_ffn(hs, moe_router[f], moe_w_gu[f], moe_w_down[f], g1, b1)
        wg, wi = ple_w_gate[i].astype(BF16), ple_w_in[i].astype(BF16)
        hp = per_layer_embed(hp, p_prompt[i].reshape(bsz * seq, D_PLE), wg, ple_b_gate[i], wi)
        hs = per_layer_embed(hs, p_sample[i].reshape(dec, D_PLE), wg, ple_b_gate[i], wi)
    (l0_p, l0_s), (l1_p, l1_s), (l2_p, l2_s), (l3_p, l3_s) = states
    return (hp.reshape(bsz, seq, d), hs.reshape(dec, 1, d),
            l0_p[0], l0_s[0], l0_p[1], l0_s[1], l0_p[2], l0_s[2],
            l1_p[0], l1_s[0],
            l2_p[0], l2_s[0], l2_p[1], l2_s[1],
            l3_p[0], l3_s[0], l3_p[1], l3_s[1], l3_p[2], l3_s[2])
```
